```python
import math
import jax
import jax.numpy as jnp
from jax import lax
import numpy as np

D_MODEL = 1024
BATCH = 4
SEQ = 4096
DEPTH = 2

S5_WIDTH = 512
S5_GROUP = 16
S5_GROUPS = S5_WIDTH // S5_GROUP
S5_STATE = 64
S5_DT_MIN = 1e-3
S5_DT_MAX = 1e-1
DA_HEADS = 4
DA_HEAD_DIM = 64
DA_V_DIM = 2 * DA_HEAD_DIM
DA_QK_WIDTH = DA_HEADS * 2 * DA_HEAD_DIM
DA_V_WIDTH = DA_HEADS * DA_V_DIM
DA_SUBLN_EPS = 1e-5
Q_BLOCK = 128
RW_HEAD = 64
RW_WIDTH = 512
RW_HEADS = RW_WIDTH // RW_HEAD
RW_DECAY_LORA = 32
RW_A_LORA = 32
RW_V_LORA = 32
RW_G_LORA = 96
RW_LN_EPS = 64e-5
RW_SHIFT_WIDTH = 3 * RW_WIDTH + RW_DECAY_LORA + RW_A_LORA + RW_G_LORA
N_BRANCH = 3
BRANCH_WIDTH = 512
IN_S5_END = S5_WIDTH
IN_Q_END = IN_S5_END + DA_QK_WIDTH
IN_K_END = IN_Q_END + DA_QK_WIDTH
IN_V_END = IN_K_END + DA_V_WIDTH
IN_RW_END = IN_V_END + RW_SHIFT_WIDTH
N_IN = IN_RW_END + N_BRANCH * D_MODEL
RW_R_END = RW_WIDTH
RW_K_END = 2 * RW_WIDTH
RW_V_END = 3 * RW_WIDTH
RW_W_END = RW_V_END + RW_DECAY_LORA
RW_A_END = RW_W_END + RW_A_LORA
D_FF = ((8 * D_MODEL + 3 * 256 - 1) // (3 * 256)) * 256
NORM_EPS = 1e-6

kernel_name = 'hybrid_s5_diffattn_rwkv7_gated'


def rms_norm(x, g, eps=NORM_EPS):
    xf = x.astype(jnp.float32)
    y = xf * lax.rsqrt(jnp.mean(xf * xf, axis=-1, keepdims=True) + eps)
    return (y * g.astype(jnp.float32)).astype(x.dtype)


def _complex_affine_combine(earlier, later):
    a1r, a1i, b1r, b1i = earlier
    a2r, a2i, b2r, b2i = later
    return (a2r * a1r - a2i * a1i,
            a2r * a1i + a2i * a1r,
            a2r * b1r - a2i * b1i + b2r,
            a2r * b1i + a2i * b1r + b2i)


def s5_mixer(u, lam_re, lam_im, log_dt, b_re, b_im, c_re, c_im, d, w_glu):
    bsz, seq, _ = u.shape
    f32 = jnp.float32
    uf = u.astype(f32)
    ug = uf.reshape(bsz, seq, S5_GROUPS, S5_GROUP)
    dt = jnp.exp(log_dt.astype(f32))[:, None]
    lr = lam_re.astype(f32)
    li = lam_im.astype(f32)
    mag = jnp.exp(lr * dt)
    ar = mag * jnp.cos(li * dt)
    ai = mag * jnp.sin(li * dt)
    den = lr * lr + li * li
    nr = ar - 1.0
    kr = (nr * lr + ai * li) / den
    ki = (ai * lr - nr * li) / den
    br = b_re.astype(f32)
    bi = b_im.astype(f32)
    bbar_r = kr[..., None] * br - ki[..., None] * bi
    bbar_i = kr[..., None] * bi + ki[..., None] * br
    bu_r = jnp.einsum('blgc,gpc->blgp', ug, bbar_r)
    bu_i = jnp.einsum('blgc,gpc->blgp', ug, bbar_i)
    a_r = jnp.broadcast_to(ar, (1, seq) + ar.shape)
    a_i = jnp.broadcast_to(ai, (1, seq) + ai.shape)
    _, _, xr, xi = lax.associative_scan(_complex_affine_combine, (a_r, a_i, bu_r, bu_i), axis=1)
    y = (jnp.einsum('blgp,gcp->blgc', xr, c_re.astype(f32))
         - jnp.einsum('blgp,gcp->blgc', xi, c_im.astype(f32)))
    y = y.reshape(bsz, seq, S5_WIDTH) + d.astype(f32) * uf
    y = jax.nn.gelu(y)
    y = y * jax.nn.sigmoid(y @ w_glu.astype(f32))
    return y.astype(u.dtype)


def diff_attention(q, k, v, q_gain, k_gain, lam_vecs, subln_g, lambda_init):
    bsz, seq, _ = q.shape
    f32 = jnp.float32
    q = rms_norm(q.reshape(bsz, seq, DA_HEADS, 2, DA_HEAD_DIM), q_gain)
    k = rms_norm(k.reshape(bsz, seq, DA_HEADS, 2, DA_HEAD_DIM), k_gain)
    qf = (q.astype(f32) * (DA_HEAD_DIM ** -0.5)).transpose(0, 2, 3, 1, 4)
    kf = k.astype(f32).transpose(0, 2, 3, 1, 4)
    vf = v.astype(f32).reshape(bsz, seq, DA_HEADS, DA_V_DIM).transpose(0, 2, 1, 3)
    lv = lam_vecs.astype(f32)
    lam = jnp.exp(jnp.sum(lv[0] * lv[1])) - jnp.exp(jnp.sum(lv[2] * lv[3])) + lambda_init
    n_blk = seq // Q_BLOCK
    qb = qf.reshape(bsz, DA_HEADS, 2, n_blk, Q_BLOCK, DA_HEAD_DIM).transpose(3, 0, 1, 2, 4, 5)
    kpos = jnp.arange(seq)

    def one_block(args):
        qi, bi = args
        s = jnp.einsum('bhcqd,bhckd->bhcqk', qi, kf)
        qpos = bi * Q_BLOCK + jnp.arange(Q_BLOCK)
        causal = kpos[None, :] <= qpos[:, None]
        p = jax.nn.softmax(jnp.where(causal, s, -jnp.inf), axis=-1)
        attn = p[:, :, 0] - lam * p[:, :, 1]
        return jnp.einsum('bhqk,bhkd->bhqd', attn, vf)

    o = lax.map(one_block, (qb, jnp.arange(n_blk)))
    o = o.transpose(1, 0, 3, 2, 4).reshape(bsz, seq, DA_HEADS, DA_V_DIM)
    o = rms_norm(o, subln_g, DA_SUBLN_EPS) * (1.0 - lambda_init)
    return o.reshape(bsz, seq, DA_V_WIDTH).astype(v.dtype)


def rwkv7_scan(r, w, k, v, a, b):
    bsz, seq, nh, n = r.shape
    xs = tuple(jnp.moveaxis(t, 1, 0) for t in (r, w, k, v, a, b))

    def step(state, inp):
        r_t, w_t, k_t, v_t, a_t, b_t = inp
        sa = jnp.einsum('bhvk,bhk->bhv', state, a_t)
        state = (state * w_t[:, :, None, :] + sa[..., None] * b_t[:, :, None, :]
                 + v_t[..., None] * k_t[:, :, None, :])
        return state, jnp.einsum('bhvk,bhk->bhv', state, r_t)

    s0 = jnp.zeros((bsz, nh, n, n), jnp.float32)
    _, ys = lax.scan(step, s0, xs)
    return jnp.moveaxis(ys, 0, 1)


def rwkv7_mixer(z, mu, w0, w2, a0, a2, g2, k_k, k_a, r_k, ln_g, ln_b, v_first, v_gate):
    bsz, seq, _ = z.shape
    f32 = jnp.float32
    zf = z.astype(f32)
    prev = jnp.pad(zf, ((0, 0), (1, 0), (0, 0)))[:, :-1, :]
    zf = zf + (prev - zf) * mu.astype(f32)
    r = zf[..., :RW_R_END]
    k = zf[..., RW_R_END:RW_K_END]
    v = zf[..., RW_K_END:RW_V_END]
    zw = zf[..., RW_V_END:RW_W_END]
    za = zf[..., RW_W_END:RW_A_END]
    zg = zf[..., RW_A_END:]
    w_log = -jax.nn.softplus(-(w0.astype(f32) + jnp.tanh(zw) @ w2.astype(f32))) - 0.5
    decay = jnp.exp(-jnp.exp(w_log))
    a = jax.nn.sigmoid(a0.astype(f32) + za @ a2.astype(f32))
    g = jax.nn.sigmoid(zg) @ g2.astype(f32)
    kk = (k * k_k.astype(f32)).reshape(bsz, seq, RW_HEADS, RW_HEAD)
    kk = kk / jnp.maximum(jnp.linalg.norm(kk, axis=-1, keepdims=True), 1e-12)
    kk = kk.reshape(bsz, seq, RW_WIDTH)
    k = k * (1.0 + (a - 1.0) * k_a.astype(f32))
    if v_gate is None:
        v_first = v
    else:
        v0, v1, v2 = v_gate
        mix = jax.nn.sigmoid(v0.astype(f32) + (v @ v1.astype(f32)) @ v2.astype(f32))
        v = v + (v_first - v) * mix
    hs = (bsz, seq, RW_HEADS, RW_HEAD)
    rh, kh, vh = r.reshape(hs), k.reshape(hs), v.reshape(hs)
    y = rwkv7_scan(rh, decay.reshape(hs), kh, vh, -kk.reshape(hs), (kk * a).reshape(hs))
    mean = jnp.mean(y, axis=-1, keepdims=True)
    var = jnp.mean(jnp.square(y - mean), axis=-1, keepdims=True)
    y = ((y - mean) * lax.rsqrt(var + RW_LN_EPS)).reshape(bsz, seq, RW_WIDTH)
    y = y * ln_g.astype(f32) + ln_b.astype(f32)
    bonus = jnp.sum(rh * kh * r_k.astype(f32), axis=-1, keepdims=True) * vh
    y = (y + bonus.reshape(bsz, seq, RW_WIDTH)) * g
    return y.astype(z.dtype), v_first


def setup_inputs(seed: int = 0) -> dict:
    key = jax.random.key(seed)
    ks = iter(jax.random.split(key, 48))
    f32 = jnp.float32

    def nrm(shape, scale):
        return scale * jax.random.normal(next(ks), shape, f32)

    G, P, C = S5_GROUPS, S5_STATE, S5_GROUP
    inp = {}
    inp['x'] = nrm((BATCH, SEQ, D_MODEL), 1.0)
    inp['norm1_g'] = 1.0 + nrm((DEPTH, D_MODEL), 0.02)
    inp['w_in'] = nrm((DEPTH, D_MODEL, N_IN), D_MODEL ** -0.5)
    inp['s5_lambda_re'] = -0.5 + nrm((DEPTH, G, P), 0.01)
    inp['s5_lambda_im'] = math.pi * jnp.arange(P, dtype=f32) + nrm((DEPTH, G, P), 0.01)
    inp['s5_log_dt'] = jax.random.uniform(next(ks), (DEPTH, G), f32, math.log(S5_DT_MIN), math.log(S5_DT_MAX))
    inp['s5_b_re'] = nrm((DEPTH, G, P, C), (2 * C) ** -0.5)
    inp['s5_b_im'] = nrm((DEPTH, G, P, C), (2 * C) ** -0.5)
    inp['s5_c_re'] = nrm((DEPTH, G, C, P), P ** -0.5)
    inp['s5_c_im'] = nrm((DEPTH, G, C, P), P ** -0.5)
    inp['s5_d'] = nrm((DEPTH, S5_WIDTH), 1.0)
    inp['s5_w_glu'] = nrm((DEPTH, S5_WIDTH, S5_WIDTH), S5_WIDTH ** -0.5)
    inp['da_q_gain'] = 1.0 + nrm((DEPTH, 2, DA_HEAD_DIM), 0.02)
    inp['da_k_gain'] = 1.0 + nrm((DEPTH, 2, DA_HEAD_DIM), 0.02)
    inp['da_lambda'] = nrm((DEPTH, 4, DA_HEAD_DIM), 0.1)
    inp['da_subln_g'] = 1.0 + nrm((DEPTH, DA_V_DIM), 0.02)
    inp['rw_mu'] = jax.random.uniform(next(ks), (DEPTH, RW_SHIFT_WIDTH), f32)
    inp['rw_w0'] = jnp.linspace(-6.5, -1.5, RW_WIDTH, dtype=f32) + nrm((DEPTH, RW_WIDTH), 0.1)
    inp['rw_w2'] = nrm((DEPTH, RW_DECAY_LORA, RW_WIDTH), 0.1)
    inp['rw_a0'] = nrm((DEPTH, RW_WIDTH), 0.1)
    inp['rw_a2'] = nrm((DEPTH, RW_A_LORA, RW_WIDTH), 0.1)
    inp['rw_g2'] = nrm((DEPTH, RW_G_LORA, RW_WIDTH), RW_G_LORA ** -0.5)
    inp['rw_k_k'] = 0.85 + nrm((DEPTH, RW_WIDTH), 0.02)
    inp['rw_k_a'] = 1.0 + nrm((DEPTH, RW_WIDTH), 0.02)
    inp['rw_r_k'] = nrm((DEPTH, RW_HEADS, RW_HEAD), 0.05)
    inp['rw_ln_g'] = 1.0 + nrm((DEPTH, RW_WIDTH), 0.02)
    inp['rw_ln_b'] = nrm((DEPTH, RW_WIDTH), 0.01)
    inp['rw_v0'] = 1.0 + nrm((DEPTH - 1, RW_WIDTH), 0.02)
    inp['rw_v1'] = nrm((DEPTH - 1, RW_WIDTH, RW_V_LORA), RW_WIDTH ** -0.5)
    inp['rw_v2'] = nrm((DEPTH - 1, RW_V_LORA, RW_WIDTH), 0.1)
    inp['w_branch'] = nrm((DEPTH, N_BRANCH, BRANCH_WIDTH, D_MODEL), BRANCH_WIDTH ** -0.5)
    inp['w_out'] = nrm((DEPTH, D_MODEL, D_MODEL), D_MODEL ** -0.5)
    inp['norm2_g'] = 1.0 + nrm((DEPTH, D_MODEL), 0.02)
    inp['w_ffn_in'] = nrm((DEPTH, D_MODEL, 2 * D_FF), D_MODEL ** -0.5)
    inp['w_ffn_out'] = nrm((DEPTH, D_FF, D_MODEL), D_FF ** -0.5)
    return inp


def reference(x, norm1_g, w_in, s5_lambda_re, s5_lambda_im, s5_log_dt, s5_b_re, s5_b_im,
              s5_c_re, s5_c_im, s5_d, s5_w_glu, da_q_gain, da_k_gain, da_lambda, da_subln_g,
              rw_mu, rw_w0, rw_w2, rw_a0, rw_a2, rw_g2, rw_k_k, rw_k_a, rw_r_k, rw_ln_g, rw_ln_b,
              rw_v0, rw_v1, rw_v2, w_branch, w_out, norm2_g, w_ffn_in, w_ffn_out):
    bsz, seq, _ = x.shape
    v_first = None
    for i in range(DEPTH):
        h = rms_norm(x, norm1_g[i])
        proj = h @ w_in[i]
        u_s5 = proj[..., :IN_S5_END]
        q = proj[..., IN_S5_END:IN_Q_END]
        k = proj[..., IN_Q_END:IN_K_END]
        v = proj[..., IN_K_END:IN_V_END]
        z_rw = proj[..., IN_V_END:IN_RW_END]
        gate_logits = proj[..., IN_RW_END:]
        y_a = s5_mixer(u_s5, s5_lambda_re[i], s5_lambda_im[i], s5_log_dt[i], s5_b_re[i], s5_b_im[i],
                       s5_c_re[i], s5_c_im[i], s5_d[i], s5_w_glu[i])
        lambda_init = 0.8 - 0.6 * math.exp(-0.3 * i)
        y_b = diff_attention(q, k, v, da_q_gain[i], da_k_gain[i], da_lambda[i], da_subln_g[i], lambda_init)
        v_gate = None if i == 0 else (rw_v0[i - 1], rw_v1[i - 1], rw_v2[i - 1])
        y_c, v_first = rwkv7_mixer(z_rw, rw_mu[i], rw_w0[i], rw_w2[i], rw_a0[i], rw_a2[i], rw_g2[i],
                                   rw_k_k[i], rw_k_a[i], rw_r_k[i], rw_ln_g[i], rw_ln_b[i], v_first, v_gate)
        branches = jnp.stack([y_a, y_b, y_c.astype(y_a.dtype)], axis=2)
        branch_out = jnp.einsum('blnc,ncd->blnd', branches, w_branch[i])
        gates = jax.nn.sigmoid(gate_logits.reshape(bsz, seq, N_BRANCH, D_MODEL))
        merged = jnp.sum(gates * branch_out, axis=2)
        x = x + (merged @ w_out[i]).astype(x.dtype)
        h2 = rms_norm(x, norm2_g[i])
        gu = h2 @ w_ffn_in[i]
        x = x + ((jax.nn.silu(gu[..., :D_FF]) * gu[..., D_FF:]) @ w_ffn_out[i]).astype(x.dtype)
    return x
```

```python
import functools
import math

import jax
import jax.numpy as jnp
from jax import lax
from jax.experimental import pallas as pl
from jax.experimental.pallas import tpu as pltpu

F32 = jnp.float32
BF16 = jnp.bfloat16

D_MODEL = 1024
DEPTH = 2
S5_WIDTH = 512
S5_GROUP = 16
S5_GROUPS = S5_WIDTH // S5_GROUP
S5_STATE = 64
S5_CHUNK = 16
DA_HEADS = 4
DA_HEAD_DIM = 64
DA_V_DIM = 2 * DA_HEAD_DIM
DA_WIDTH = DA_HEADS * DA_V_DIM
DA_SUBLN_EPS = 1e-5
RW_HEAD = 64
RW_WIDTH = 512
RW_HEADS = RW_WIDTH // RW_HEAD
RW_DECAY_LORA = 32
RW_A_LORA = 32
RW_G_LORA = 96
RW_TAIL = RW_DECAY_LORA + RW_A_LORA + RW_G_LORA
RW_SHIFT_WIDTH = 3 * RW_WIDTH + RW_TAIL
RW_LN_EPS = 64e-5
RW_CHUNK = 64
N_BRANCH = 3
D_FF = 2816
NORM_EPS = 1e-6
NEG_BIG = -1e30

VMEM_LIMIT = 56 * 1024 * 1024


def _cparams(n_axes):
    return pltpu.CompilerParams(dimension_semantics=("arbitrary",) * n_axes, vmem_limit_bytes=VMEM_LIMIT)


def _const_spec(shape):
    nd = len(shape)
    return pl.BlockSpec(shape, lambda *_: (0,) * nd, pipeline_mode=pl.Buffered(1))


def _mm(a, b):
    return jnp.dot(a.astype(BF16), b.astype(BF16), preferred_element_type=F32)


def _mm_nt(a, b):
    return lax.dot_general(a.astype(BF16), b.astype(BF16), (((1,), (1,)), ((), ())), preferred_element_type=F32)


def _mm_tn(a, b):
    return lax.dot_general(a.astype(BF16), b.astype(BF16), (((0,), (0,)), ((), ())), preferred_element_type=F32)


def _split2(x):
    hi = x.astype(BF16)
    lo = (x - hi.astype(F32)).astype(BF16)
    return hi, lo


def _split3(x):
    hi = x.astype(BF16)
    r = x - hi.astype(F32)
    mid = r.astype(BF16)
    lo = (r - mid.astype(F32)).astype(BF16)
    return hi, mid, lo


def _mm_sel_right(x, sel):
    hi, mid, lo = _split3(x)
    d = lambda p: jnp.dot(p, sel, preferred_element_type=F32)
    return d(hi) + d(mid) + d(lo)


def _mm_sel_left(sel, x):
    hi, mid, lo = _split3(x)
    d = lambda p: jnp.dot(sel, p, preferred_element_type=F32)
    return d(hi) + d(mid) + d(lo)


def _mm_split(x, w_hi, w_lo):
    x_hi, x_lo = _split2(x)
    d = lambda p, q: jnp.dot(p, q, preferred_element_type=F32)
    return d(x_hi, w_hi) + d(x_hi, w_lo) + d(x_lo, w_hi)


def _sigmoid(x):
    return 1.0 / (1.0 + jnp.exp(-x))


def _group_ones(width, group):
    r = jnp.arange(width) // group
    return (r[:, None] == r[None, :]).astype(BF16)


def _in_proj_kernel(x_ref, g_ref, ws5_ref, wq_ref, wk_ref, wv_ref, wrw_ref, wgate_ref, e64_ref, qg_ref, kg_ref,
                    us5_ref, q_ref, k_ref, v_ref, zrw_ref, gl_ref):
    x = x_ref[...]
    h = x * lax.rsqrt(jnp.mean(x * x, axis=-1, keepdims=True) + NORM_EPS) * g_ref[...]
    h = h.astype(BF16)
    dot = lambda w_ref: jnp.dot(h, w_ref[...], preferred_element_type=F32)
    us5_ref[...] = dot(ws5_ref)

    def qk_norm(t, gain_ref):
        ms = _mm_sel_right(t * t, e64_ref[...]) * (1.0 / DA_HEAD_DIM)
        return t * lax.rsqrt(ms + NORM_EPS) * gain_ref[...]

    q_ref[...] = qk_norm(dot(wq_ref), qg_ref).astype(BF16)
    k_ref[...] = qk_norm(dot(wk_ref), kg_ref).astype(BF16)
    v_ref[...] = dot(wv_ref).astype(BF16)
    zrw_ref[...] = dot(wrw_ref)
    gl_ref[...] = dot(wgate_ref)


def _in_proj(x2, g, w, q_gain, k_gain, tm):
    t = x2.shape[0]
    w = w.astype(BF16)
    o = 0
    parts = []
    for width in (S5_WIDTH, DA_WIDTH, DA_WIDTH, DA_WIDTH, RW_SHIFT_WIDTH, N_BRANCH * D_MODEL):
        parts.append(w[:, o:o + width])
        o += width
    e64 = _group_ones(DA_WIDTH, DA_HEAD_DIM)
    qg = jnp.tile(q_gain.reshape(-1), DA_HEADS)[None, :] * (DA_HEAD_DIM ** -0.5)
    kg = jnp.tile(k_gain.reshape(-1), DA_HEADS)[None, :]
    row = lambda width: pl.BlockSpec((tm, width), lambda i: (i, 0))
    consts = [g[None, :]] + parts + [e64, qg, kg]
    out_widths = (S5_WIDTH, DA_WIDTH, DA_WIDTH, DA_WIDTH, RW_SHIFT_WIDTH, N_BRANCH * D_MODEL)
    out_dtypes = (F32, BF16, BF16, BF16, F32, F32)
    return pl.pallas_call(
        _in_proj_kernel,
        grid=(t // tm,),
        in_specs=[row(D_MODEL)] + [_const_spec(c.shape) for c in consts],
        out_specs=[row(wd) for wd in out_widths],
        out_shape=[jax.ShapeDtypeStruct((t, wd), dt) for wd, dt in zip(out_widths, out_dtypes)],
        compiler_params=_cparams(1),
        name="in_proj",
    )(x2, *consts)


def _s5_tables(lam_re, lam_im, log_dt, b_re, b_im, c_re, c_im, n_double):
    n = S5_CHUNK
    dt = jnp.exp(log_dt)[:, None]
    den = lam_re * lam_re + lam_im * lam_im

    def a_pow(tau):
        mag = jnp.exp(lam_re * dt * tau)
        return mag * jnp.cos(lam_im * dt * tau), mag * jnp.sin(lam_im * dt * tau)

    ar, ai = a_pow(1.0)
    nr = ar - 1.0
    kr = (nr * lam_re + ai * lam_im) / den
    ki = (ai * lam_re - nr * lam_im) / den
    bbar_r = kr[..., None] * b_re - ki[..., None] * b_im
    bbar_i = kr[..., None] * b_im + ki[..., None] * b_re
    taus = jnp.arange(n + 1, dtype=lam_re.dtype)
    pr, pi = jax.vmap(a_pow)(taus)
    hp = lax.Precision.HIGHEST
    abr = pr[..., None] * bbar_r[None] - pi[..., None] * bbar_i[None]
    abi = pr[..., None] * bbar_i[None] + pi[..., None] * bbar_r[None]
    ktau = (jnp.einsum('gdp,ngpc->ngdc', c_re, abr, precision=hp)
            - jnp.einsum('gdp,ngpc->ngdc', c_im, abi, precision=hp))
    lag = jnp.arange(n)[None, :] - jnp.arange(n)[:, None]
    toep = ktau[jnp.clip(lag, 0, n)]
    toep = jnp.where((lag >= 0)[:, :, None, None, None], toep, 0.0)
    g = lam_re.shape[0]
    toep = toep.transpose(2, 0, 4, 1, 3).reshape(g, n * S5_GROUP, n * S5_GROUP)
    rev = n - 1 - jnp.arange(n)
    b_end = jnp.concatenate([abr[rev], abi[rev]], axis=2)
    b_end = b_end.transpose(1, 0, 3, 2).reshape(g, n * S5_GROUP, 2 * S5_STATE)
    cr = c_re[None] * pr[1:, :, None, :] - c_im[None] * pi[1:, :, None, :]
    ci = c_re[None] * pi[1:, :, None, :] + c_im[None] * pr[1:, :, None, :]
    c_in = jnp.concatenate([cr, -ci], axis=3)
    c_in = c_in.transpose(1, 3, 0, 2).reshape(g, 2 * S5_STATE, n * S5_GROUP)
    rows = []
    for k in range(n_double):
        sr, si = a_pow(float(n * 2 ** k))
        rows += [jnp.concatenate([sr, sr], axis=1), jnp.concatenate([-si, si], axis=1)]
    step = jnp.stack(rows, axis=1)
    return toep.astype(BF16), b_end.astype(BF16), c_in.astype(BF16), step


def _s5_kernel(u_ref, toep_ref, bend_ref, cin_ref, step_ref, y_ref, *, n_batch, n_chunk, n_double):
    u = u_ref[0].astype(BF16)
    z = jnp.dot(u, bend_ref[0], preferred_element_type=F32)
    row = lax.broadcasted_iota(jnp.int32, (n_chunk, 2 * S5_STATE), 0)
    prev = []
    for b in range(n_batch):
        x = z[b * n_chunk:(b + 1) * n_chunk]
        for k in range(n_double):
            s = 2 ** k
            sh = jnp.where(row >= s, pltpu.roll(x, s, 0), 0.0)
            x = x + step_ref[0, 2 * k:2 * k + 1, :] * sh + step_ref[0, 2 * k + 1:2 * k + 2, :] * pltpu.roll(sh, S5_STATE, 1)
        prev.append(jnp.where(row >= 1, pltpu.roll(x, 1, 0), 0.0))
    xin = jnp.concatenate(prev, axis=0).astype(BF16)
    y_ref[0] = (jnp.dot(u, toep_ref[0], preferred_element_type=F32)
                + jnp.dot(xin, cin_ref[0], preferred_element_type=F32))


def _s5_scan(u, tables, n_batch, seq):
    toep, b_end, c_in, step = tables
    n_chunk = seq // S5_CHUNK
    n_double = step.shape[1] // 2
    rows = n_batch * n_chunk
    lanes = S5_CHUNK * S5_GROUP
    ug = u.reshape(rows, S5_CHUNK, S5_GROUPS, S5_GROUP).transpose(2, 0, 1, 3).reshape(S5_GROUPS, rows, lanes)
    per_group = lambda a: pl.BlockSpec((1,) + a.shape[1:], lambda g: (g, 0, 0))
    y = pl.pallas_call(
        functools.partial(_s5_kernel, n_batch=n_batch, n_chunk=n_chunk, n_double=n_double),
        grid=(S5_GROUPS,),
        in_specs=[per_group(ug), per_group(toep), per_group(b_end), per_group(c_in), per_group(step)],
        out_specs=pl.BlockSpec((1, rows, lanes), lambda g: (g, 0, 0)),
        out_shape=jax.ShapeDtypeStruct((S5_GROUPS, rows, lanes), F32),
        compiler_params=_cparams(1),
        name="s5_chunk",
    )(ug, toep, b_end, c_in, step)
    return y.reshape(S5_GROUPS, rows, S5_CHUNK, S5_GROUP).transpose(1, 2, 0, 3).reshape(rows * S5_CHUNK, S5_WIDTH)


def _attn_kernel(lam_ref, sg_ref, q_ref, k_ref, v_ref, o_ref, m_sc, l_sc, acc_sc, *, tq, lambda_init):
    qi = pl.program_id(2)
    q = q_ref[...]
    lane = lax.broadcasted_iota(jnp.int32, q.shape, 1)
    zero = jnp.zeros_like(q)
    q2 = jnp.concatenate([jnp.where(lane < DA_HEAD_DIM, q, zero), jnp.where(lane >= DA_HEAD_DIM, q, zero)], axis=0)
    m_sc[...] = jnp.full(m_sc.shape, NEG_BIG, F32)
    l_sc[...] = jnp.zeros(l_sc.shape, F32)
    acc_sc[...] = jnp.zeros(acc_sc.shape, F32)

    def block(j, masked):
        start = pl.multiple_of(j * tq, tq)
        kb = k_ref[pl.ds(start, tq), :]
        vb = v_ref[pl.ds(start, tq), :]
        s = lax.dot_general(q2, kb, (((1,), (1,)), ((), ())), preferred_element_type=F32)
        if masked:
            r = lax.broadcasted_iota(jnp.int32, s.shape, 0)
            c = lax.broadcasted_iota(jnp.int32, s.shape, 1)
            qpos = jnp.where(r >= tq, r - tq, r)
            s = jnp.where(c <= qpos, s, NEG_BIG)
        m_prev = m_sc[...]
        m_next = jnp.maximum(m_prev, jnp.max(s, axis=1, keepdims=True))
        alpha = jnp.exp(m_prev - m_next)
        p = jnp.exp(s - m_next[:, :1])
        l_sc[...] = alpha * l_sc[...] + jnp.sum(p, axis=1, keepdims=True)
        acc_sc[...] = alpha * acc_sc[...] + jnp.dot(p.astype(BF16), vb, preferred_element_type=F32)
        m_sc[...] = m_next

    def body(j, carry):
        block(j, False)
        return carry

    lax.fori_loop(0, qi, body, 0)
    block(qi, True)

    lv = lam_ref[...]
    lam = (jnp.exp(jnp.sum(lv[0:1] * lv[1:2], axis=1, keepdims=True))
           - jnp.exp(jnp.sum(lv[2:3] * lv[3:4], axis=1, keepdims=True)) + lambda_init)
    o_all = acc_sc[...] / l_sc[...]
    o = o_all[:tq] - lam * o_all[tq:]
    o = o * lax.rsqrt(jnp.mean(o * o, axis=-1, keepdims=True) + DA_SUBLN_EPS) * sg_ref[...] * (1.0 - lambda_init)
    o_ref[...] = o.astype(o_ref.dtype)


def _diff_attn(q, k, v, lam_vecs, subln_g, lambda_init, n_batch, seq, tq):
    nq = seq // tq
    q_spec = pl.BlockSpec((tq, DA_V_DIM), lambda b, h, i: (b * nq + i, h))
    kv_spec = pl.BlockSpec((seq, DA_V_DIM), lambda b, h, i: (b, h))
    return pl.pallas_call(
        functools.partial(_attn_kernel, tq=tq, lambda_init=lambda_init),
        grid=(n_batch, DA_HEADS, nq),
        in_specs=[_const_spec(lam_vecs.shape), _const_spec((1, DA_V_DIM)), q_spec, kv_spec, kv_spec],
        out_specs=q_spec,
        out_shape=jax.ShapeDtypeStruct((n_batch * seq, DA_WIDTH), BF16),
        scratch_shapes=[pltpu.VMEM((2 * tq, DA_V_DIM), F32)] * 3,
        compiler_params=_cparams(3),
        name="diff_attn",
    )(lam_vecs, subln_g[None, :], q, k, v)


def _rw_prep_kernel(*refs, tm, seq, gated):
    if gated:
        (z_ref, zp_ref, vf_ref, mu_ref, w0_ref, a0_ref, kk_ref, ka_ref, rk_ref, w2h_ref, w2l_ref, a2h_ref, a2l_ref,
         g2h_ref, g2l_ref, e64_ref, tri_ref, last_ref, v0_ref, v1h_ref, v1l_ref, v2h_ref, v2l_ref,
         rh_ref, ah_ref, bh_ref, kh_ref, bt_ref, kt_ref, v_ref, gc_ref, bonus_ref, g_ref) = refs
    else:
        (z_ref, zp_ref, mu_ref, w0_ref, a0_ref, kk_ref, ka_ref, rk_ref, w2h_ref, w2l_ref, a2h_ref, a2l_ref,
         g2h_ref, g2l_ref, e64_ref, tri_ref, last_ref,
         rh_ref, ah_ref, bh_ref, kh_ref, bt_ref, kt_ref, v_ref, gc_ref, bonus_ref, g_ref, vfirst_ref) = refs
    i = pl.program_id(0)
    z = z_ref[...]
    first = jnp.where((i * tm) % seq == 0, 0.0, 1.0) * zp_ref[7:8, :]
    row = lax.broadcasted_iota(jnp.int32, z.shape, 0)
    prev = jnp.where(row == 0, first, pltpu.roll(z, 1, 0))
    zf = z + (prev - z) * mu_ref[...]
    r = zf[:, :RW_WIDTH]
    k = zf[:, RW_WIDTH:2 * RW_WIDTH]
    v = zf[:, 2 * RW_WIDTH:3 * RW_WIDTH]
    tail = zf[:, 3 * RW_WIDTH:]
    w_in = w0_ref[...] + _mm_split(jnp.tanh(tail), w2h_ref[...], w2l_ref[...])
    softplus = jnp.maximum(-w_in, 0.0) + jnp.log(1.0 + jnp.exp(-jnp.abs(w_in)))
    logw = -jnp.exp(-softplus - 0.5)
    a = _sigmoid(a0_ref[...] + _mm_split(tail, a2h_ref[...], a2l_ref[...]))
    g_ref[...] = _mm_split(_sigmoid(tail), g2h_ref[...], g2l_ref[...])
    kk = k * kk_ref[...]
    norm = jnp.sqrt(_mm_sel_right(kk * kk, e64_ref[...]))
    kk = kk / jnp.maximum(norm, 1e-12)
    k = k * (1.0 + (a - 1.0) * ka_ref[...])
    if gated:
        lo = _mm_split(v, v1h_ref[...], v1l_ref[...])
        mix = _sigmoid(v0_ref[...] + _mm_split(lo, v2h_ref[...], v2l_ref[...]))
        v = v + (vf_ref[...] - v) * mix
    else:
        vfirst_ref[...] = v
    bonus_ref[...] = _mm_sel_right(r * k * rk_ref[...], e64_ref[...]) * v
    cs = _mm_sel_left(tri_ref[...], logw)
    cs_end = _mm_sel_left(last_ref[...], cs)
    inv = jnp.exp(-cs)
    to_end = jnp.exp(cs_end - cs)
    b = kk * a
    rh_ref[...] = (r * jnp.exp(cs)).astype(BF16)
    ah_ref[...] = (-kk * jnp.exp(cs - logw)).astype(BF16)
    bh_ref[...] = (b * inv).astype(BF16)
    kh_ref[...] = (k * inv).astype(BF16)
    bt_ref[...] = (b * to_end).astype(BF16)
    kt_ref[...] = (k * to_end).astype(BF16)
    v_ref[...] = v.astype(BF16)
    gc_ref[...] = jnp.exp(cs_end)


def _pad_rows(w, start):
    return jnp.zeros((RW_TAIL, w.shape[1]), w.dtype).at[start:start + w.shape[0]].set(w)


def _hi_lo(w):
    hi = w.astype(BF16)
    return hi, (w - hi.astype(w.dtype)).astype(BF16)


def _rw_prep(z, p, v_first, seq, tm):
    t = z.shape[0]
    gated = v_first is not None
    idx = jnp.arange(tm)
    same_chunk = (idx[:, None] // RW_CHUNK) == (idx[None, :] // RW_CHUNK)
    tri = (same_chunk & (idx[None, :] <= idx[:, None])).astype(BF16)
    last = (same_chunk & (idx[None, :] % RW_CHUNK == RW_CHUNK - 1)).astype(BF16)
    vec = lambda a: a[None, :]
    consts = [vec(p['mu']), vec(p['w0']), vec(p['a0']), vec(p['k_k']), vec(p['k_a']), vec(p['r_k'].reshape(-1)),
              *_hi_lo(_pad_rows(p['w2'], 0)), *_hi_lo(_pad_rows(p['a2'], RW_DECAY_LORA)),
              *_hi_lo(_pad_rows(p['g2'], RW_DECAY_LORA + RW_A_LORA)),
              _group_ones(RW_WIDTH, RW_HEAD), tri, last]
    if gated:
        consts += [vec(p['v0']), *_hi_lo(p['v1']), *_hi_lo(p['v2'])]
    row = lambda width: pl.BlockSpec((tm, width), lambda i: (i, 0))
    zp_spec = pl.BlockSpec((8, RW_SHIFT_WIDTH), lambda i: (jnp.maximum(i * (tm // 8) - 1, 0), 0))
    in_specs = [row(RW_SHIFT_WIDTH), zp_spec] + ([row(RW_WIDTH)] if gated else []) + [_const_spec(c.shape) for c in consts]
    args = [z, z] + ([v_first] if gated else []) + consts
    n_out = 10 if gated else 11
    dtypes = [BF16] * 7 + [F32] * 3 + ([] if gated else [F32])
    outs = pl.pallas_call(
        functools.partial(_rw_prep_kernel, tm=tm, seq=seq, gated=gated),
        grid=(t // tm,),
        in_specs=in_specs,
        out_specs=[row(RW_WIDTH)] * n_out,
        out_shape=[jax.ShapeDtypeStruct((t, RW_WIDTH), dt) for dt in dtypes],
        compiler_params=_cparams(1),
        name="rw_prep",
    )(*args)
    return outs


def _rw_scan_kernel(rh_ref, ah_ref, bh_ref, kh_ref, bt_ref, kt_ref, v_ref, gc_ref, y_ref, h_sc):
    @pl.when(pl.program_id(1) == 0)
    def _():
        h_sc[...] = jnp.zeros(h_sc.shape, F32)

    n = RW_CHUNK
    ri = lax.broadcasted_iota(jnp.int32, (n, n), 0)
    ci = lax.broadcasted_iota(jnp.int32, (n, n), 1)
    strict = ci < ri
    incl = ci <= ri
    eye = ci == ri
    gc_last = gc_ref[n - 1:n, :]
    for h in range(RW_HEADS):
        sl = slice(h * RW_HEAD, (h + 1) * RW_HEAD)
        rh, ah, bh, kh = rh_ref[:, sl], ah_ref[:, sl], bh_ref[:, sl], kh_ref[:, sl]
        bt, kt, v = bt_ref[:, sl], kt_ref[:, sl], v_ref[:, sl]
        h0 = h_sc[h]
        ar = jnp.concatenate([ah, rh], axis=0)
        gb = _mm_nt(ar, bh)
        gk = _mm_nt(ar, kh)
        l_ab = jnp.where(strict, gb[:n], 0.0)
        l_ak = jnp.where(strict, gk[:n], 0.0)
        m_rb = jnp.where(incl, gb[n:], 0.0)
        m_rk = jnp.where(incl, gk[n:], 0.0)
        tinv = jnp.where(eye, 1.0, l_ab)
        pw = l_ab
        for _ in range(5):
            pw = _mm(pw, pw)
            tinv = tinv + _mm(tinv, pw)
        h0b = h0.astype(BF16)
        u = _mm(tinv, _mm(ah, h0b) + _mm(l_ak, v))
        y_ref[:, sl] = _mm(rh, h0b) + _mm(m_rb, u) + _mm(m_rk, v)
        decay = jnp.where(eye, jnp.broadcast_to(gc_last[:, sl], (n, n)), 0.0)
        h_sc[h] = (jnp.dot(decay, h0, preferred_element_type=F32, precision=lax.Precision.HIGHEST)
                   + _mm_tn(bt, u) + _mm_tn(kt, v))


def _rw_scan(ops, n_batch, seq):
    n_chunk = seq // RW_CHUNK
    spec = pl.BlockSpec((RW_CHUNK, RW_WIDTH), lambda b, c: (b * n_chunk + c, 0))
    return pl.pallas_call(
        _rw_scan_kernel,
        grid=(n_batch, n_chunk),
        in_specs=[spec] * 8,
        out_specs=spec,
        out_shape=jax.ShapeDtypeStruct((n_batch * seq, RW_WIDTH), F32),
        scratch_shapes=[pltpu.VMEM((RW_HEADS, RW_HEAD, RW_HEAD), F32)],
        compiler_params=_cparams(2),
        name="rw_scan",
    )(*ops)


def _merge_kernel(x_ref, ys5_ref, us5_ref, yb_ref, yrw_ref, bonus_ref, g_ref, gl_ref, d_ref, wglu_ref, lng_ref, lnb_ref,
                  e64_ref, wb_ref, wout_ref, o_ref):
    ya = ys5_ref[...] + d_ref[...] * us5_ref[...]
    ya = 0.5 * ya * (1.0 + jnp.tanh(math.sqrt(2.0 / math.pi) * (ya + 0.044715 * (ya * ya * ya))))
    ya = ya * _sigmoid(_mm(ya, wglu_ref[...]))
    y = yrw_ref[...]
    mean = _mm_sel_right(y, e64_ref[...]) * (1.0 / RW_HEAD)
    yc = y - mean
    var = _mm_sel_right(yc * yc, e64_ref[...]) * (1.0 / RW_HEAD)
    yc = yc * lax.rsqrt(var + RW_LN_EPS) * lng_ref[...] + lnb_ref[...]
    yc = (yc + bonus_ref[...]) * g_ref[...]
    merged = jnp.zeros((x_ref.shape[0], D_MODEL), F32)
    for n, br in enumerate((ya, yb_ref[...], yc)):
        gate = _sigmoid(gl_ref[:, n * D_MODEL:(n + 1) * D_MODEL])
        merged = merged + gate * _mm(br, wb_ref[n])
    o_ref[...] = x_ref[...] + _mm(merged, wout_ref[...])


def _merge(x2, ys5, us5, yb, yrw, bonus, g, gl, s5_d, w_glu, ln_g, ln_b, w_branch, w_out, tm):
    t = x2.shape[0]
    consts = [s5_d[None, :], w_glu.astype(BF16), ln_g[None, :], ln_b[None, :], _group_ones(RW_WIDTH, RW_HEAD),
              w_branch.astype(BF16), w_out.astype(BF16)]
    row = lambda width: pl.BlockSpec((tm, width), lambda i: (i, 0))
    widths = [D_MODEL, S5_WIDTH, S5_WIDTH, DA_WIDTH, RW_WIDTH, RW_WIDTH, RW_WIDTH, N_BRANCH * D_MODEL]
    return pl.pallas_call(
        _merge_kernel,
        grid=(t // tm,),
        in_specs=[row(wd) for wd in widths] + [_const_spec(c.shape) for c in consts],
        out_specs=row(D_MODEL),
        out_shape=jax.ShapeDtypeStruct((t, D_MODEL), F32),
        compiler_params=_cparams(1),
        name="merge",
    )(x2, ys5, us5, yb, yrw, bonus, g, gl, *consts)


def _ffn_kernel(x_ref, g_ref, win_ref, wout_ref, o_ref):
    x = x_ref[...]
    h = (x * lax.rsqrt(jnp.mean(x * x, axis=-1, keepdims=True) + NORM_EPS) * g_ref[...]).astype(BF16)
    gate = jnp.dot(h, win_ref[:, :D_FF], preferred_element_type=F32)
    up = jnp.dot(h, win_ref[:, D_FF:], preferred_element_type=F32)
    act = gate * _sigmoid(gate) * up
    o_ref[...] = x + _mm(act, wout_ref[...])


def _ffn(x2, g, w_in, w_out, tm):
    t = x2.shape[0]
    consts = [g[None, :], w_in.astype(BF16), w_out.astype(BF16)]
    row = pl.BlockSpec((tm, D_MODEL), lambda i: (i, 0))
    return pl.pallas_call(
        _ffn_kernel,
        grid=(t // tm,),
        in_specs=[row] + [_const_spec(c.shape) for c in consts],
        out_specs=row,
        out_shape=jax.ShapeDtypeStruct((t, D_MODEL), F32),
        compiler_params=_cparams(1),
        name="ffn",
    )(x2, *consts)


def _layer(x2, i, n_batch, seq, p, v_first, tm):
    us5, q, k, v, zrw, gl = _in_proj(x2, p['norm1_g'], p['w_in'], p['da_q_gain'], p['da_k_gain'], tm)
    n_double = max(1, (seq // S5_CHUNK - 1).bit_length())
    tables = _s5_tables(p['s5_lambda_re'], p['s5_lambda_im'], p['s5_log_dt'], p['s5_b_re'], p['s5_b_im'],
                        p['s5_c_re'], p['s5_c_im'], n_double)
    ys5 = _s5_scan(us5, tables, n_batch, seq)
    lambda_init = 0.8 - 0.6 * math.exp(-0.3 * i)
    yb = _diff_attn(q, k, v, p['da_lambda'], p['da_subln_g'], lambda_init, n_batch, seq, min(256, seq))
    rw = {name[3:]: val for name, val in p.items() if name.startswith('rw_')}
    outs = _rw_prep(zrw, rw, v_first, seq, tm)
    if v_first is None:
        v_first = outs[10]
    yrw = _rw_scan(outs[:8], n_batch, seq)
    x2 = _merge(x2, ys5, us5, yb, yrw, outs[8], outs[9], gl, p['s5_d'], p['s5_w_glu'], p['rw_ln_g'], p['rw_ln_b'],
                p['w_branch'], p['w_out'], tm)
    x2 = _ffn(x2, p['norm2_g'], p['w_ffn_in'], p['w_ffn_out'], tm)
    return x2, v_first


def kernel(x, norm1_g, w_in, s5_lambda_re, s5_lambda_im, s5_log_dt, s5_b_re, s5_b_im, s5_c_re, s5_c_im, s5_d, s5_w_glu, da_q_gain, da_k_gain, da_lambda, da_subln_g, rw_mu, rw_w0, rw_w2, rw_a0, rw_a2, rw_g2, rw_k_k, rw_k_a, rw_r_k, rw_ln_g, rw_ln_b, rw_v0, rw_v1, rw_v2, w_branch, w_out, norm2_g, w_ffn_in, w_ffn_out):
    n_batch, seq, _ = x.shape
    per_layer = dict(norm1_g=norm1_g, w_in=w_in, s5_lambda_re=s5_lambda_re, s5_lambda_im=s5_lambda_im,
                     s5_log_dt=s5_log_dt, s5_b_re=s5_b_re, s5_b_im=s5_b_im, s5_c_re=s5_c_re, s5_c_im=s5_c_im,
                     s5_d=s5_d, s5_w_glu=s5_w_glu, da_q_gain=da_q_gain, da_k_gain=da_k_gain, da_lambda=da_lambda,
                     da_subln_g=da_subln_g, rw_mu=rw_mu, rw_w0=rw_w0, rw_w2=rw_w2, rw_a0=rw_a0, rw_a2=rw_a2,
                     rw_g2=rw_g2, rw_k_k=rw_k_k, rw_k_a=rw_k_a, rw_r_k=rw_r_k, rw_ln_g=rw_ln_g, rw_ln_b=rw_ln_b,
                     w_branch=w_branch, w_out=w_out, norm2_g=norm2_g, w_ffn_in=w_ffn_in, w_ffn_out=w_ffn_out)
    tm = min(256, seq)
    x2 = x.reshape(n_batch * seq, D_MODEL)
    v_first = None
    for i in range(w_in.shape[0]):
        p = {name: val[i] for name, val in per_layer.items()}
        if i > 0:
            p.update(rw_v0=rw_v0[i - 1], rw_v1=rw_v1[i - 1], rw_v2=rw_v2[i - 1])
        x2, v_first = _layer(x2, i, n_batch, seq, p, v_first, tm)
    return x2.reshape(x.shape)
```

```python
import functools
import math

import jax
import jax.numpy as jnp
from jax import lax
from jax.experimental import pallas as pl
from jax.experimental.pallas import tpu as pltpu

F32 = jnp.float32
BF16 = jnp.bfloat16

D_MODEL = 1024
DEPTH = 2
S5_WIDTH = 512
S5_GROUP = 16
S5_GROUPS = S5_WIDTH // S5_GROUP
S5_STATE = 64
S5_CHUNK = 16
DA_HEADS = 4
DA_HEAD_DIM = 64
DA_V_DIM = 2 * DA_HEAD_DIM
DA_WIDTH = DA_HEADS * DA_V_DIM
DA_SUBLN_EPS = 1e-5
RW_HEAD = 64
RW_WIDTH = 512
RW_HEADS = RW_WIDTH // RW_HEAD
RW_DECAY_LORA = 32
RW_A_LORA = 32
RW_G_LORA = 96
RW_TAIL = RW_DECAY_LORA + RW_A_LORA + RW_G_LORA
RW_SHIFT_WIDTH = 3 * RW_WIDTH + RW_TAIL
RW_LN_EPS = 64e-5
RW_CHUNK = 64
N_BRANCH = 3
D_FF = 2816
NORM_EPS = 1e-6
NEG_BIG = -1e30

VMEM_LIMIT = 56 * 1024 * 1024


def _cparams(n_axes):
    return pltpu.CompilerParams(dimension_semantics=("arbitrary",) * n_axes, vmem_limit_bytes=VMEM_LIMIT)


def _const_spec(shape):
    nd = len(shape)
    return pl.BlockSpec(shape, lambda *_: (0,) * nd, pipeline_mode=pl.Buffered(1))


def _mm(a, b):
    return jnp.dot(a.astype(BF16), b.astype(BF16), preferred_element_type=F32)


def _mm_nt(a, b):
    return lax.dot_general(a.astype(BF16), b.astype(BF16), (((1,), (1,)), ((), ())), preferred_element_type=F32)


def _mm_tn(a, b):
    return lax.dot_general(a.astype(BF16), b.astype(BF16), (((0,), (0,)), ((), ())), preferred_element_type=F32)


def _split2(x):
    hi = x.astype(BF16)
    lo = (x - hi.astype(F32)).astype(BF16)
    return hi, lo


def _split3(x):
    hi = x.astype(BF16)
    r = x - hi.astype(F32)
    mid = r.astype(BF16)
    lo = (r - mid.astype(F32)).astype(BF16)
    return hi, mid, lo


def _mm_sel_right(x, sel):
    hi, mid, lo = _split3(x)
    d = lambda p: jnp.dot(p, sel, preferred_element_type=F32)
    return d(hi) + d(mid) + d(lo)


def _mm_sel_left(sel, x):
    hi, mid, lo = _split3(x)
    d = lambda p: jnp.dot(sel, p, preferred_element_type=F32)
    return d(hi) + d(mid) + d(lo)


def _mm_split(x, w_hi, w_lo):
    x_hi, x_lo = _split2(x)
    d = lambda p, q: jnp.dot(p, q, preferred_element_type=F32)
    return d(x_hi, w_hi) + d(x_hi, w_lo) + d(x_lo, w_hi)


def _sigmoid(x):
    return 1.0 / (1.0 + jnp.exp(-x))


def _group_ones(width, group):
    r = jnp.arange(width) // group
    return (r[:, None] == r[None, :]).astype(BF16)


def _in_proj_kernel(x_ref, g_ref, ws5_ref, wq_ref, wk_ref, wvt_ref, wrw_ref, wgate_ref, e64_ref, qg_ref, kg_ref,
                    us5_ref, q_ref, k_ref, vt_ref, zrw_ref, gl_ref):
    x = x_ref[...]
    h = x * lax.rsqrt(jnp.mean(x * x, axis=-1, keepdims=True) + NORM_EPS) * g_ref[...]
    h = h.astype(BF16)
    dot = lambda w_ref: jnp.dot(h, w_ref[...], preferred_element_type=F32)
    us5_ref[...] = dot(ws5_ref)

    def qk_norm(t, gain_ref):
        ms = _mm_sel_right(t * t, e64_ref[...]) * (1.0 / DA_HEAD_DIM)
        return t * lax.rsqrt(ms + NORM_EPS) * gain_ref[...]

    q_ref[...] = qk_norm(dot(wq_ref), qg_ref).astype(BF16)
    k_ref[...] = qk_norm(dot(wk_ref), kg_ref).astype(BF16)
    vt_ref[0] = lax.dot_general(wvt_ref[...], h, (((1,), (1,)), ((), ())), preferred_element_type=F32).astype(BF16)
    zrw_ref[...] = dot(wrw_ref)
    gl_ref[...] = dot(wgate_ref)


def _in_proj(x2, g, w, q_gain, k_gain, tm):
    t = x2.shape[0]
    w = w.astype(BF16)
    o = 0
    parts = []
    for width in (S5_WIDTH, DA_WIDTH, DA_WIDTH, DA_WIDTH, RW_SHIFT_WIDTH, N_BRANCH * D_MODEL):
        parts.append(w[:, o:o + width])
        o += width
    parts[3] = parts[3].T
    e64 = _group_ones(DA_WIDTH, DA_HEAD_DIM)
    qg = jnp.tile(q_gain.reshape(-1), DA_HEADS)[None, :] * (DA_HEAD_DIM ** -0.5)
    kg = jnp.tile(k_gain.reshape(-1), DA_HEADS)[None, :]
    row = lambda width: pl.BlockSpec((tm, width), lambda i: (i, 0))
    consts = [g[None, :]] + parts + [e64, qg, kg]
    out_specs = [row(S5_WIDTH), row(DA_WIDTH), row(DA_WIDTH), pl.BlockSpec((1, DA_WIDTH, tm), lambda i: (i, 0, 0)),
                 row(RW_SHIFT_WIDTH), row(N_BRANCH * D_MODEL)]
    out_shape = [jax.ShapeDtypeStruct((t, S5_WIDTH), F32), jax.ShapeDtypeStruct((t, DA_WIDTH), BF16),
                 jax.ShapeDtypeStruct((t, DA_WIDTH), BF16), jax.ShapeDtypeStruct((t // tm, DA_WIDTH, tm), BF16),
                 jax.ShapeDtypeStruct((t, RW_SHIFT_WIDTH), F32), jax.ShapeDtypeStruct((t, N_BRANCH * D_MODEL), F32)]
    return pl.pallas_call(
        _in_proj_kernel,
        grid=(t // tm,),
        in_specs=[row(D_MODEL)] + [_const_spec(c.shape) for c in consts],
        out_specs=out_specs,
        out_shape=out_shape,
        compiler_params=_cparams(1),
        name="in_proj",
    )(x2, *consts)


def _s5_tables(lam_re, lam_im, log_dt, b_re, b_im, c_re, c_im, n_double):
    n = S5_CHUNK
    dt = jnp.exp(log_dt)[:, None]
    den = lam_re * lam_re + lam_im * lam_im

    def a_pow(tau):
        mag = jnp.exp(lam_re * dt * tau)
        return mag * jnp.cos(lam_im * dt * tau), mag * jnp.sin(lam_im * dt * tau)

    ar, ai = a_pow(1.0)
    nr = ar - 1.0
    kr = (nr * lam_re + ai * lam_im) / den
    ki = (ai * lam_re - nr * lam_im) / den
    bbar_r = kr[..., None] * b_re - ki[..., None] * b_im
    bbar_i = kr[..., None] * b_im + ki[..., None] * b_re
    taus = jnp.arange(n + 1, dtype=lam_re.dtype)
    pr, pi = jax.vmap(a_pow)(taus)
    hp = lax.Precision.HIGHEST
    abr = pr[..., None] * bbar_r[None] - pi[..., None] * bbar_i[None]
    abi = pr[..., None] * bbar_i[None] + pi[..., None] * bbar_r[None]
    ktau = (jnp.einsum('gdp,ngpc->ngdc', c_re, abr, precision=hp)
            - jnp.einsum('gdp,ngpc->ngdc', c_im, abi, precision=hp))
    lag = jnp.arange(n)[None, :] - jnp.arange(n)[:, None]
    toep = ktau[jnp.clip(lag, 0, n)]
    toep = jnp.where((lag >= 0)[:, :, None, None, None], toep, 0.0)
    g = lam_re.shape[0]
    toep = toep.transpose(2, 0, 4, 1, 3).reshape(g, n * S5_GROUP, n * S5_GROUP)
    rev = n - 1 - jnp.arange(n)
    b_end = jnp.concatenate([abr[rev], abi[rev]], axis=2)
    b_end = b_end.transpose(1, 0, 3, 2).reshape(g, n * S5_GROUP, 2 * S5_STATE)
    cr = c_re[None] * pr[1:, :, None, :] - c_im[None] * pi[1:, :, None, :]
    ci = c_re[None] * pi[1:, :, None, :] + c_im[None] * pr[1:, :, None, :]
    c_in = jnp.concatenate([cr, -ci], axis=3)
    c_in = c_in.transpose(1, 3, 0, 2).reshape(g, 2 * S5_STATE, n * S5_GROUP)
    rows = []
    for k in range(n_double):
        sr, si = a_pow(float(n * 2 ** k))
        rows += [jnp.concatenate([sr, sr], axis=1), jnp.concatenate([-si, si], axis=1)]
    step = jnp.stack(rows, axis=1)
    return toep.astype(BF16), b_end.astype(BF16), c_in.astype(BF16), step


def _s5_kernel(u_ref, toep_ref, bend_ref, cin_ref, step_ref, y_ref, *, n_batch, n_chunk, n_double):
    u = u_ref[0].astype(BF16)
    z = jnp.dot(u, bend_ref[0], preferred_element_type=F32)
    row = lax.broadcasted_iota(jnp.int32, (n_chunk, 2 * S5_STATE), 0)
    prev = []
    for b in range(n_batch):
        x = z[b * n_chunk:(b + 1) * n_chunk]
        for k in range(n_double):
            s = 2 ** k
            sh = jnp.where(row >= s, pltpu.roll(x, s, 0), 0.0)
            x = x + step_ref[0, 2 * k:2 * k + 1, :] * sh + step_ref[0, 2 * k + 1:2 * k + 2, :] * pltpu.roll(sh, S5_STATE, 1)
        prev.append(jnp.where(row >= 1, pltpu.roll(x, 1, 0), 0.0))
    xin = jnp.concatenate(prev, axis=0).astype(BF16)
    y_ref[0] = (jnp.dot(u, toep_ref[0], preferred_element_type=F32)
                + jnp.dot(xin, cin_ref[0], preferred_element_type=F32))


def _s5_scan(u, tables, n_batch, seq):
    toep, b_end, c_in, step = tables
    n_chunk = seq // S5_CHUNK
    n_double = step.shape[1] // 2
    rows = n_batch * n_chunk
    lanes = S5_CHUNK * S5_GROUP
    ug = u.reshape(rows, S5_CHUNK, S5_GROUPS, S5_GROUP).transpose(2, 0, 1, 3).reshape(S5_GROUPS, rows, lanes)
    per_group = lambda a: pl.BlockSpec((1,) + a.shape[1:], lambda g: (g, 0, 0))
    y = pl.pallas_call(
        functools.partial(_s5_kernel, n_batch=n_batch, n_chunk=n_chunk, n_double=n_double),
        grid=(S5_GROUPS,),
        in_specs=[per_group(ug), per_group(toep), per_group(b_end), per_group(c_in), per_group(step)],
        out_specs=pl.BlockSpec((1, rows, lanes), lambda g: (g, 0, 0)),
        out_shape=jax.ShapeDtypeStruct((S5_GROUPS, rows, lanes), F32),
        compiler_params=_cparams(1),
        name="s5_chunk",
    )(ug, toep, b_end, c_in, step)
    return y.reshape(S5_GROUPS, rows, S5_CHUNK, S5_GROUP).transpose(1, 2, 0, 3).reshape(rows * S5_CHUNK, S5_WIDTH)


def _attn_kernel(lam_ref, sg_ref, q_ref, k_ref, vt_ref, o_ref, m_sc, l_sc, acc_sc, *, tq, lambda_init):
    qi = pl.program_id(2)
    q = q_ref[...]
    lane = lax.broadcasted_iota(jnp.int32, q.shape, 1)
    zero = jnp.zeros_like(q)
    q2 = jnp.concatenate([jnp.where(lane < DA_HEAD_DIM, q, zero), jnp.where(lane >= DA_HEAD_DIM, q, zero)], axis=0)
    m_sc[...] = jnp.full(m_sc.shape, NEG_BIG, F32)
    l_sc[...] = jnp.zeros(l_sc.shape, F32)
    acc_sc[...] = jnp.zeros(acc_sc.shape, F32)

    def scores(j):
        kb = k_ref[pl.ds(pl.multiple_of(j * tq, tq), tq), :]
        return lax.dot_general(kb, q2, (((1,), (1,)), ((), ())), preferred_element_type=F32)

    def update(s, j, masked):
        if masked:
            key = lax.broadcasted_iota(jnp.int32, s.shape, 0)
            r = lax.broadcasted_iota(jnp.int32, s.shape, 1)
            s = jnp.where(key <= jnp.where(r >= tq, r - tq, r), s, NEG_BIG)
        m_prev = m_sc[...]
        m_next = jnp.maximum(m_prev, jnp.max(s, axis=0, keepdims=True))
        alpha = jnp.exp(m_prev - m_next)
        p = jnp.exp(s - m_next)
        l_sc[...] = alpha * l_sc[...] + jnp.sum(p, axis=0, keepdims=True)
        acc_sc[...] = alpha * acc_sc[...] + jnp.dot(vt_ref[j], p.astype(BF16), preferred_element_type=F32)
        m_sc[...] = m_next

    def body(j, s_cur):
        s_next = scores(j + 1)
        update(s_cur, j, False)
        return s_next

    update(lax.fori_loop(0, qi, body, scores(0)), qi, True)

    lv = lam_ref[...]
    lam = (jnp.exp(jnp.sum(lv[0:1] * lv[1:2], axis=1, keepdims=True))
           - jnp.exp(jnp.sum(lv[2:3] * lv[3:4], axis=1, keepdims=True)) + lambda_init)
    o_all = acc_sc[...] / l_sc[...]
    o = (o_all[:, :tq] - lam * o_all[:, tq:]).T
    o = o * lax.rsqrt(jnp.mean(o * o, axis=-1, keepdims=True) + DA_SUBLN_EPS) * sg_ref[...] * (1.0 - lambda_init)
    o_ref[...] = o.astype(o_ref.dtype)


def _diff_attn(q, k, vt, lam_vecs, subln_g, lambda_init, n_batch, seq, tq):
    nq = seq // tq
    q_spec = pl.BlockSpec((tq, DA_V_DIM), lambda b, h, i: (b * nq + i, h))
    k_spec = pl.BlockSpec((seq, DA_V_DIM), lambda b, h, i: (b, h))
    vt_spec = pl.BlockSpec((nq, DA_V_DIM, tq), lambda b, h, i: (b, h, 0))
    return pl.pallas_call(
        functools.partial(_attn_kernel, tq=tq, lambda_init=lambda_init),
        grid=(n_batch, DA_HEADS, nq),
        in_specs=[_const_spec(lam_vecs.shape), _const_spec((1, DA_V_DIM)), q_spec, k_spec, vt_spec],
        out_specs=q_spec,
        out_shape=jax.ShapeDtypeStruct((n_batch * seq, DA_WIDTH), BF16),
        scratch_shapes=[pltpu.VMEM((1, 2 * tq), F32), pltpu.VMEM((1, 2 * tq), F32),
                        pltpu.VMEM((DA_V_DIM, 2 * tq), F32)],
        compiler_params=_cparams(3),
        name="diff_attn",
    )(lam_vecs, subln_g[None, :], q, k, vt)


def _rw_prep_kernel(*refs, tm, seq, gated):
    if gated:
        (z_ref, zp_ref, vf_ref, mu_ref, w0_ref, a0_ref, kk_ref, ka_ref, rk_ref, w2h_ref, w2l_ref, a2h_ref, a2l_ref,
         g2h_ref, g2l_ref, e64_ref, tri_ref, last_ref, v0_ref, v1h_ref, v1l_ref, v2h_ref, v2l_ref,
         rh_ref, ah_ref, bh_ref, kh_ref, bt_ref, kt_ref, v_ref, gc_ref, bonus_ref, g_ref) = refs
    else:
        (z_ref, zp_ref, mu_ref, w0_ref, a0_ref, kk_ref, ka_ref, rk_ref, w2h_ref, w2l_ref, a2h_ref, a2l_ref,
         g2h_ref, g2l_ref, e64_ref, tri_ref, last_ref,
         rh_ref, ah_ref, bh_ref, kh_ref, bt_ref, kt_ref, v_ref, gc_ref, bonus_ref, g_ref, vfirst_ref) = refs
    i = pl.program_id(0)
    z = z_ref[...]
    first = jnp.where((i * tm) % seq == 0, 0.0, 1.0) * zp_ref[7:8, :]
    row = lax.broadcasted_iota(jnp.int32, z.shape, 0)
    prev = jnp.where(row == 0, first, pltpu.roll(z, 1, 0))
    zf = z + (prev - z) * mu_ref[...]
    r = zf[:, :RW_WIDTH]
    k = zf[:, RW_WIDTH:2 * RW_WIDTH]
    v = zf[:, 2 * RW_WIDTH:3 * RW_WIDTH]
    tail = zf[:, 3 * RW_WIDTH:]
    w_in = w0_ref[...] + _mm_split(jnp.tanh(tail), w2h_ref[...], w2l_ref[...])
    softplus = jnp.maximum(-w_in, 0.0) + jnp.log(1.0 + jnp.exp(-jnp.abs(w_in)))
    logw = -jnp.exp(-softplus - 0.5)
    a = _sigmoid(a0_ref[...] + _mm_split(tail, a2h_ref[...], a2l_ref[...]))
    g_ref[...] = _mm_split(_sigmoid(tail), g2h_ref[...], g2l_ref[...])
    kk = k * kk_ref[...]
    norm = jnp.sqrt(_mm_sel_right(kk * kk, e64_ref[...]))
    kk = kk / jnp.maximum(norm, 1e-12)
    k = k * (1.0 + (a - 1.0) * ka_ref[...])
    if gated:
        lo = _mm_split(v, v1h_ref[...], v1l_ref[...])
        mix = _sigmoid(v0_ref[...] + _mm_split(lo, v2h_ref[...], v2l_ref[...]))
        v = v + (vf_ref[...] - v) * mix
    else:
        vfirst_ref[...] = v
    bonus_ref[...] = _mm_sel_right(r * k * rk_ref[...], e64_ref[...]) * v
    cs = _mm_sel_left(tri_ref[...], logw)
    cs_end = _mm_sel_left(last_ref[...], cs)
    inv = jnp.exp(-cs)
    to_end = jnp.exp(cs_end - cs)
    b = kk * a
    rh_ref[...] = (r * jnp.exp(cs)).astype(BF16)
    ah_ref[...] = (-kk * jnp.exp(cs - logw)).astype(BF16)
    bh_ref[...] = (b * inv).astype(BF16)
    kh_ref[...] = (k * inv).astype(BF16)
    bt_ref[...] = (b * to_end).astype(BF16)
    kt_ref[...] = (k * to_end).astype(BF16)
    v_ref[...] = v.astype(BF16)
    gc_ref[...] = jnp.exp(cs_end)


def _pad_rows(w, start):
    return jnp.zeros((RW_TAIL, w.shape[1]), w.dtype).at[start:start + w.shape[0]].set(w)


def _hi_lo(w):
    hi = w.astype(BF16)
    return hi, (w - hi.astype(w.dtype)).astype(BF16)


def _rw_prep(z, p, v_first, seq, tm):
    t = z.shape[0]
    gated = v_first is not None
    idx = jnp.arange(tm)
    same_chunk = (idx[:, None] // RW_CHUNK) == (idx[None, :] // RW_CHUNK)
    tri = (same_chunk & (idx[None, :] <= idx[:, None])).astype(BF16)
    last = (same_chunk & (idx[None, :] % RW_CHUNK == RW_CHUNK - 1)).astype(BF16)
    vec = lambda a: a[None, :]
    consts = [vec(p['mu']), vec(p['w0']), vec(p['a0']), vec(p['k_k']), vec(p['k_a']), vec(p['r_k'].reshape(-1)),
              *_hi_lo(_pad_rows(p['w2'], 0)), *_hi_lo(_pad_rows(p['a2'], RW_DECAY_LORA)),
              *_hi_lo(_pad_rows(p['g2'], RW_DECAY_LORA + RW_A_LORA)),
              _group_ones(RW_WIDTH, RW_HEAD), tri, last]
    if gated:
        consts += [vec(p['v0']), *_hi_lo(p['v1']), *_hi_lo(p['v2'])]
    row = lambda width: pl.BlockSpec((tm, width), lambda i: (i, 0))
    zp_spec = pl.BlockSpec((8, RW_SHIFT_WIDTH), lambda i: (jnp.maximum(i * (tm // 8) - 1, 0), 0))
    in_specs = [row(RW_SHIFT_WIDTH), zp_spec] + ([row(RW_WIDTH)] if gated else []) + [_const_spec(c.shape) for c in consts]
    args = [z, z] + ([v_first] if gated else []) + consts
    n_out = 10 if gated else 11
    dtypes = [BF16] * 7 + [F32] * 3 + ([] if gated else [F32])
    outs = pl.pallas_call(
        functools.partial(_rw_prep_kernel, tm=tm, seq=seq, gated=gated),
        grid=(t // tm,),
        in_specs=in_specs,
        out_specs=[row(RW_WIDTH)] * n_out,
        out_shape=[jax.ShapeDtypeStruct((t, RW_WIDTH), dt) for dt in dtypes],
        compiler_params=_cparams(1),
        name="rw_prep",
    )(*args)
    return outs


def _rw_intra_kernel(rh_ref, ah_ref, bh_ref, kh_ref, bt_ref, kt_ref, v_ref, gc_ref,
                     wr_ref, mb_ref, uloc_ref, oloc_ref, kvloc_ref, gcol_ref, *, n_sub):
    n = RW_CHUNK
    ri = lax.broadcasted_iota(jnp.int32, (n, n), 0)
    ci = lax.broadcasted_iota(jnp.int32, (n, n), 1)
    strict = ci < ri
    incl = ci <= ri
    eye = ci == ri
    eye_b = eye.astype(BF16)
    ones_b = jnp.ones((n, n), BF16)
    items = [(c, h) for c in range(n_sub) for h in range(RW_HEADS)]
    rows = lambda c: slice(c * n, (c + 1) * n)
    lanes = lambda h: slice(h * RW_HEAD, (h + 1) * RW_HEAD)
    ld = lambda ref: [ref[rows(c), lanes(h)] for c, h in items]
    rh, ah, bh, kh, bt, kt, v = (ld(r) for r in (rh_ref, ah_ref, bh_ref, kh_ref, bt_ref, kt_ref, v_ref))
    ar = [jnp.concatenate([a, r], axis=0) for a, r in zip(ah, rh)]
    gb = [_mm_nt(x, y) for x, y in zip(ar, bh)]
    gk = [_mm_nt(x, y) for x, y in zip(ar, kh)]
    l_ab = [jnp.where(strict, g[:n], 0.0) for g in gb]
    m_rb =[jnp.where(incl, g[n:], 0.0).astype(BF16) for g in gb]
    lm_k = [jnp.concatenate([jnp.where(strict, g[:n], 0.0), jnp.where(incl, g[n:], 0.0)], axis=0).astype(BF16) for g in gk]
    akv = [_mm(x, y) for x, y in zip(lm_k, v)]
    tinv = [jnp.where(eye, 1.0, x) for x in l_ab]
    l_ab = [x.astype(BF16) for x in l_ab]
    pw = [_mm(x, x) for x in l_ab]
    for _ in range(4):
        pw_b = [x.astype(BF16) for x in pw]
        res = [_mm(jnp.concatenate([p, t.astype(BF16)], axis=0), p) for p, t in zip(pw_b, tinv)]
        pw = [r[:n] for r in res]
        tinv = [t + r[n:] for t, r in zip(tinv, res)]
    tinv = [(t + _mm(t, p)).astype(BF16) for t, p in zip(tinv, pw)]
    w = [_mm(t, a) for t, a in zip(tinv, ah)]
    uloc = [_mm(t, x[:n]) for t, x in zip(tinv, akv)]
    kvloc = [_mm_tn(x, y) for x, y in zip(kt, v)]
    bt_t = [_mm_tn(x, eye_b) for x in bt]
    for i, (c, h) in enumerate(items):
        decay = jnp.where(eye, jnp.broadcast_to(gc_ref[(c + 1) * n - 1:(c + 1) * n, lanes(h)], (n, n)), 0.0)
        gcol_ref[rows(c), lanes(h)] = _mm_sel_right(decay, ones_b)
        wr_ref[2 * c * n:(2 * c + 1) * n, lanes(h)] = w[i].astype(BF16)
        wr_ref[(2 * c + 1) * n:(2 * c + 2) * n, lanes(h)] = rh[i]
        mb_ref[2 * c * n:(2 * c + 1) * n, lanes(h)] = m_rb[i]
        mb_ref[(2 * c + 1) * n:(2 * c + 2) * n, lanes(h)] = bt_t[i].astype(BF16)
        uloc_ref[rows(c), lanes(h)] = uloc[i]
        oloc_ref[rows(c), lanes(h)] = akv[i][n:]
        kvloc_ref[rows(c), lanes(h)] = kvloc[i]


def _rw_seq_kernel(wr_ref, mb_ref, uloc_ref, oloc_ref, kvloc_ref, gcol_ref, y_ref, h_sc, *, n_batch):
    @pl.when(pl.program_id(0) == 0)
    def _():
        h_sc[...] = jnp.zeros(h_sc.shape, F32)

    n = RW_CHUNK
    items = [(b, h) for b in range(n_batch) for h in range(RW_HEADS)]
    lanes = lambda h: slice(h * RW_HEAD, (h + 1) * RW_HEAD)
    h0 = [h_sc[i] for i in range(len(items))]
    s1 = [_mm(wr_ref[b, :, lanes(h)], x) for (b, h), x in zip(items, h0)]
    u = [x[:n] + uloc_ref[b, :, lanes(h)] for (b, h), x in zip(items, s1)]
    s2 = [_mm(mb_ref[b, :, lanes(h)], x) for (b, h), x in zip(items, u)]
    for i, (b, h) in enumerate(items):
        y_ref[b, :, lanes(h)] = s1[i][n:] + s2[i][:n] + oloc_ref[b, :, lanes(h)]
        h_sc[i] = gcol_ref[b, :, lanes(h)] * h0[i] + s2[i][n:] + kvloc_ref[b, :, lanes(h)]


def _rw_scan(ops, n_batch, seq):
    t = n_batch * seq
    n_sub = 4
    rows = n_sub * RW_CHUNK
    spec = pl.BlockSpec((rows, RW_WIDTH), lambda i: (i, 0))
    spec2 = pl.BlockSpec((2 * rows, RW_WIDTH), lambda i: (i, 0))
    wr, mb, uloc, oloc, kvloc, gcol = pl.pallas_call(
        functools.partial(_rw_intra_kernel, n_sub=n_sub),
        grid=(t // rows,),
        in_specs=[spec] * 8,
        out_specs=[spec2, spec2, spec, spec, spec, spec],
        out_shape=[jax.ShapeDtypeStruct((2 * t, RW_WIDTH), BF16)] * 2 + [jax.ShapeDtypeStruct((t, RW_WIDTH), F32)] * 4,
        compiler_params=_cparams(1),
        name="rw_intra",
    )(*ops)
    n_chunk = seq // RW_CHUNK
    b3 = lambda a: a.reshape(n_batch, -1, RW_WIDTH)
    cspec = pl.BlockSpec((n_batch, RW_CHUNK, RW_WIDTH), lambda c: (0, c, 0))
    cspec2 = pl.BlockSpec((n_batch, 2 * RW_CHUNK, RW_WIDTH), lambda c: (0, c, 0))
    y = pl.pallas_call(
        functools.partial(_rw_seq_kernel, n_batch=n_batch),
        grid=(n_chunk,),
        in_specs=[cspec2, cspec2, cspec, cspec, cspec, cspec],
        out_specs=cspec,
        out_shape=jax.ShapeDtypeStruct((n_batch, seq, RW_WIDTH), F32),
        scratch_shapes=[pltpu.VMEM((n_batch * RW_HEADS, RW_HEAD, RW_HEAD), F32)],
        compiler_params=_cparams(1),
        name="rw_seq",
    )(b3(wr), b3(mb), b3(uloc), b3(oloc), b3(kvloc), b3(gcol))
    return y.reshape(t, RW_WIDTH)


def _merge_kernel(x_ref, ys5_ref, us5_ref, yb_ref, yrw_ref, bonus_ref, g_ref, gl_ref, d_ref, wglu_ref, lng_ref, lnb_ref,
                  e64_ref, wb_ref, wout_ref, o_ref):
    ya = ys5_ref[...] + d_ref[...] * us5_ref[...]
    ya = 0.5 * ya * (1.0 + jnp.tanh(math.sqrt(2.0 / math.pi) * (ya + 0.044715 * (ya * ya * ya))))
    ya = ya * _sigmoid(_mm(ya, wglu_ref[...]))
    y = yrw_ref[...]
    mean = _mm_sel_right(y, e64_ref[...]) * (1.0 / RW_HEAD)
    yc = y - mean
    var = _mm_sel_right(yc * yc, e64_ref[...]) * (1.0 / RW_HEAD)
    yc = yc * lax.rsqrt(var + RW_LN_EPS) * lng_ref[...] + lnb_ref[...]
    yc = (yc + bonus_ref[...]) * g_ref[...]
    merged = jnp.zeros((x_ref.shape[0], D_MODEL), F32)
    for n, br in enumerate((ya, yb_ref[...], yc)):
        gate = _sigmoid(gl_ref[:, n * D_MODEL:(n + 1) * D_MODEL])
        merged = merged + gate * _mm(br, wb_ref[n])
    o_ref[...] = x_ref[...] + _mm(merged, wout_ref[...])


def _merge(x2, ys5, us5, yb, yrw, bonus, g, gl, s5_d, w_glu, ln_g, ln_b, w_branch, w_out, tm):
    t = x2.shape[0]
    consts = [s5_d[None, :], w_glu.astype(BF16), ln_g[None, :], ln_b[None, :], _group_ones(RW_WIDTH, RW_HEAD),
              w_branch.astype(BF16), w_out.astype(BF16)]
    row = lambda width: pl.BlockSpec((tm, width), lambda i: (i, 0))
    widths = [D_MODEL, S5_WIDTH, S5_WIDTH, DA_WIDTH, RW_WIDTH, RW_WIDTH, RW_WIDTH, N_BRANCH * D_MODEL]
    return pl.pallas_call(
        _merge_kernel,
        grid=(t // tm,),
        in_specs=[row(wd) for wd in widths] + [_const_spec(c.shape) for c in consts],
        out_specs=row(D_MODEL),
        out_shape=jax.ShapeDtypeStruct((t, D_MODEL), F32),
        compiler_params=_cparams(1),
        name="merge",
    )(x2, ys5, us5, yb, yrw, bonus, g, gl, *consts)


def _ffn_kernel(x_ref, g_ref, win_ref, wout_ref, o_ref):
    x = x_ref[...]
    h = (x * lax.rsqrt(jnp.mean(x * x, axis=-1, keepdims=True) + NORM_EPS) * g_ref[...]).astype(BF16)
    gate = jnp.dot(h, win_ref[:, :D_FF], preferred_element_type=F32)
    up = jnp.dot(h, win_ref[:, D_FF:], preferred_element_type=F32)
    act = gate * _sigmoid(gate) * up
    o_ref[...] = x + _mm(act, wout_ref[...])


def _ffn(x2, g, w_in, w_out, tm):
    t = x2.shape[0]
    consts = [g[None, :], w_in.astype(BF16), w_out.astype(BF16)]
    row = pl.BlockSpec((tm, D_MODEL), lambda i: (i, 0))
    return pl.pallas_call(
        _ffn_kernel,
        grid=(t // tm,),
        in_specs=[row] + [_const_spec(c.shape) for c in consts],
        out_specs=row,
        out_shape=jax.ShapeDtypeStruct((t, D_MODEL), F32),
        compiler_params=_cparams(1),
        name="ffn",
    )(x2, *consts)


def _layer(x2, i, n_batch, seq, p, v_first, tm):
    us5, q, k, v, zrw, gl = _in_proj(x2, p['norm1_g'], p['w_in'], p['da_q_gain'], p['da_k_gain'], tm)
    n_double = max(1, (seq // S5_CHUNK - 1).bit_length())
    tables = _s5_tables(p['s5_lambda_re'], p['s5_lambda_im'], p['s5_log_dt'], p['s5_b_re'], p['s5_b_im'],
                        p['s5_c_re'], p['s5_c_im'], n_double)
    ys5 = _s5_scan(us5, tables, n_batch, seq)
    lambda_init = 0.8 - 0.6 * math.exp(-0.3 * i)
    yb = _diff_attn(q, k, v, p['da_lambda'], p['da_subln_g'], lambda_init, n_batch, seq, tm)
    rw = {name[3:]: val for name, val in p.items() if name.startswith('rw_')}
    outs = _rw_prep(zrw, rw, v_first, seq, tm)
    if v_first is None:
        v_first = outs[10]
    yrw = _rw_scan(outs[:8], n_batch, seq)
    x2 = _merge(x2, ys5, us5, yb, yrw, outs[8], outs[9], gl, p['s5_d'], p['s5_w_glu'], p['rw_ln_g'], p['rw_ln_b'],
                p['w_branch'], p['w_out'], tm)
    x2 = _ffn(x2, p['norm2_g'], p['w_ffn_in'], p['w_ffn_out'], tm)
    return x2, v_first


def kernel(x, norm1_g, w_in, s5_lambda_re, s5_lambda_im, s5_log_dt, s5_b_re, s5_b_im, s5_c_re, s5_c_im, s5_d, s5_w_glu, da_q_gain, da_k_gain, da_lambda, da_subln_g, rw_mu, rw_w0, rw_w2, rw_a0, rw_a2, rw_g2, rw_k_k, rw_k_a, rw_r_k, rw_ln_g, rw_ln_b, rw_v0, rw_v1, rw_v2, w_branch, w_out, norm2_g, w_ffn_in, w_ffn_out):
    n_batch, seq, _ = x.shape
    per_layer = dict(norm1_g=norm1_g, w_in=w_in, s5_lambda_re=s5_lambda_re, s5_lambda_im=s5_lambda_im,
                     s5_log_dt=s5_log_dt, s5_b_re=s5_b_re, s5_b_im=s5_b_im, s5_c_re=s5_c_re, s5_c_im=s5_c_im,
                     s5_d=s5_d, s5_w_glu=s5_w_glu, da_q_gain=da_q_gain, da_k_gain=da_k_gain, da_lambda=da_lambda,
                     da_subln_g=da_subln_g, rw_mu=rw_mu, rw_w0=rw_w0, rw_w2=rw_w2, rw_a0=rw_a0, rw_a2=rw_a2,
                     rw_g2=rw_g2, rw_k_k=rw_k_k, rw_k_a=rw_k_a, rw_r_k=rw_r_k, rw_ln_g=rw_ln_g, rw_ln_b=rw_ln_b,
                     w_branch=w_branch, w_out=w_out, norm2_g=norm2_g, w_ffn_in=w_ffn_in, w_ffn_out=w_ffn_out)
    tm = min(256, seq)
    x2 = x.reshape(n_batch * seq, D_MODEL)
    v_first = None
    for i in range(w_in.shape[0]):
        p = {name: val[i] for name, val in per_layer.items()}
        if i > 0:
            p.update(rw_v0=rw_v0[i - 1], rw_v1=rw_v1[i - 1], rw_v2=rw_v2[i - 1])
        x2, v_first = _layer(x2, i, n_batch, seq, p, v_first, tm)
    return x2.reshape(x.shape)
```

```python
import functools
import math

import jax
import jax.numpy as jnp
from jax import lax
from jax.experimental import pallas as pl
from jax.experimental.pallas import tpu as pltpu

F32 = jnp.float32
BF16 = jnp.bfloat16

D_MODEL = 1024
DEPTH = 2
S5_WIDTH = 512
S5_GROUP = 16
S5_GROUPS = S5_WIDTH // S5_GROUP
S5_STATE = 64
S5_CHUNK = 16
S5_SLAB = 8
DA_HEADS = 4
DA_HEAD_DIM = 64
DA_V_DIM = 2 * DA_HEAD_DIM
DA_WIDTH = DA_HEADS * DA_V_DIM
DA_SUBLN_EPS = 1e-5
RW_HEAD = 64
RW_WIDTH = 512
RW_HEADS = RW_WIDTH // RW_HEAD
RW_DECAY_LORA = 32
RW_A_LORA = 32
RW_G_LORA = 96
RW_TAIL = RW_DECAY_LORA + RW_A_LORA + RW_G_LORA
RW_SHIFT_WIDTH = 3 * RW_WIDTH + RW_TAIL
RW_LN_EPS = 64e-5
RW_CHUNK = 64
N_BRANCH = 3
D_FF = 2816
NORM_EPS = 1e-6
NEG_BIG = -1e30
ATTN_TQ = 512
ATTN_STRIP = 256

VMEM_LIMIT = 56 * 1024 * 1024


def _cparams(n_axes):
    return pltpu.CompilerParams(dimension_semantics=("arbitrary",) * n_axes, vmem_limit_bytes=VMEM_LIMIT)


def _const_spec(shape):
    nd = len(shape)
    return pl.BlockSpec(shape, lambda *_: (0,) * nd, pipeline_mode=pl.Buffered(1))


def _mm(a, b):
    return jnp.dot(a.astype(BF16), b.astype(BF16), preferred_element_type=F32)


def _mm_nt(a, b):
    return lax.dot_general(a.astype(BF16), b.astype(BF16), (((1,), (1,)), ((), ())), preferred_element_type=F32)


def _mm_tn(a, b):
    return lax.dot_general(a.astype(BF16), b.astype(BF16), (((0,), (0,)), ((), ())), preferred_element_type=F32)


def _split2(x):
    hi = x.astype(BF16)
    lo = (x - hi.astype(F32)).astype(BF16)
    return hi, lo


def _split3(x):
    hi = x.astype(BF16)
    r = x - hi.astype(F32)
    mid = r.astype(BF16)
    lo = (r - mid.astype(F32)).astype(BF16)
    return hi, mid, lo


def _mm_sel_right(x, sel):
    hi, mid, lo = _split3(x)
    d = lambda p: jnp.dot(p, sel, preferred_element_type=F32)
    return d(hi) + d(mid) + d(lo)


def _mm_sel_left(sel, x):
    hi, mid, lo = _split3(x)
    d = lambda p: jnp.dot(sel, p, preferred_element_type=F32)
    return d(hi) + d(mid) + d(lo)


def _mm_split(x, w_hi, w_lo):
    x_hi, x_lo = _split2(x)
    d = lambda p, q: jnp.dot(p, q, preferred_element_type=F32)
    return d(x_hi, w_hi) + d(x_hi, w_lo) + d(x_lo, w_hi)


def _sigmoid(x):
    return 1.0 / (1.0 + jnp.exp(-x))


def _group_ones(width, group):
    r = jnp.arange(width) // group
    return (r[:, None] == r[None, :]).astype(BF16)


def _in_proj_kernel(x_ref, g_ref, ws5_ref, wq_ref, wk_ref, wvt_ref, wrw_ref, wgate_ref, e64_ref, qg_ref, kg_ref,
                    us5_ref, q_ref, k_ref, vt_ref, zrw_ref, gl_ref):
    x = x_ref[...]
    h = x * lax.rsqrt(jnp.mean(x * x, axis=-1, keepdims=True) + NORM_EPS) * g_ref[...]
    h = h.astype(BF16)
    dot = lambda w_ref: jnp.dot(h, w_ref[...], preferred_element_type=F32)
    us5_ref[...] = dot(ws5_ref)

    def qk_norm(t, gain_ref):
        ms = _mm_sel_right(t * t, e64_ref[...]) * (1.0 / DA_HEAD_DIM)
        return t * lax.rsqrt(ms + NORM_EPS) * gain_ref[...]

    q_ref[...] = qk_norm(dot(wq_ref), qg_ref).astype(BF16)
    k_ref[...] = qk_norm(dot(wk_ref), kg_ref).astype(BF16)
    vt_ref[0] = lax.dot_general(wvt_ref[...], h, (((1,), (1,)), ((), ())), preferred_element_type=F32).astype(BF16)
    zrw_ref[...] = dot(wrw_ref)
    gl_ref[...] = dot(wgate_ref)


def _in_proj(x2, g, w, q_gain, k_gain, tm):
    t = x2.shape[0]
    w = w.astype(BF16)
    o = 0
    parts = []
    for width in (S5_WIDTH, DA_WIDTH, DA_WIDTH, DA_WIDTH, RW_SHIFT_WIDTH, N_BRANCH * D_MODEL):
        parts.append(w[:, o:o + width])
        o += width
    parts[3] = parts[3].T
    e64 = _group_ones(DA_WIDTH, DA_HEAD_DIM)
    qg = jnp.tile(q_gain.reshape(-1), DA_HEADS)[None, :] * (DA_HEAD_DIM ** -0.5 * math.log2(math.e))
    kg = jnp.tile(k_gain.reshape(-1), DA_HEADS)[None, :]
    row = lambda width: pl.BlockSpec((tm, width), lambda i: (i, 0))
    consts = [g[None, :]] + parts + [e64, qg, kg]
    out_specs = [row(S5_WIDTH), row(DA_WIDTH), row(DA_WIDTH), pl.BlockSpec((1, DA_WIDTH, tm), lambda i: (i, 0, 0)),
                 row(RW_SHIFT_WIDTH), row(N_BRANCH * D_MODEL)]
    out_shape = [jax.ShapeDtypeStruct((t, S5_WIDTH), F32), jax.ShapeDtypeStruct((t, DA_WIDTH), BF16),
                 jax.ShapeDtypeStruct((t, DA_WIDTH), BF16), jax.ShapeDtypeStruct((t // tm, DA_WIDTH, tm), BF16),
                 jax.ShapeDtypeStruct((t, RW_SHIFT_WIDTH), F32), jax.ShapeDtypeStruct((t, N_BRANCH * D_MODEL), F32)]
    return pl.pallas_call(
        _in_proj_kernel,
        grid=(t // tm,),
        in_specs=[row(D_MODEL)] + [_const_spec(c.shape) for c in consts],
        out_specs=out_specs,
        out_shape=out_shape,
        compiler_params=_cparams(1),
        name="in_proj",
    )(x2, *consts)


def _s5_tables(lam_re, lam_im, log_dt, b_re, b_im, c_re, c_im, n_double):
    n = S5_CHUNK
    dt = jnp.exp(log_dt)[:, None]
    den = lam_re * lam_re + lam_im * lam_im

    def a_pow(tau):
        mag = jnp.exp(lam_re * dt * tau)
        return mag * jnp.cos(lam_im * dt * tau), mag * jnp.sin(lam_im * dt * tau)

    ar, ai = a_pow(1.0)
    nr = ar - 1.0
    kr = (nr * lam_re + ai * lam_im) / den
    ki = (ai * lam_re - nr * lam_im) / den
    bbar_r = kr[..., None] * b_re - ki[..., None] * b_im
    bbar_i = kr[..., None] * b_im + ki[..., None] * b_re
    taus = jnp.arange(n + 1, dtype=lam_re.dtype)
    pr, pi = jax.vmap(a_pow)(taus)
    hp = lax.Precision.HIGHEST
    abr = pr[..., None] * bbar_r[None] - pi[..., None] * bbar_i[None]
    abi = pr[..., None] * bbar_i[None] + pi[..., None] * bbar_r[None]
    ktau = (jnp.einsum('gdp,ngpc->ngdc', c_re, abr, precision=hp)
            - jnp.einsum('gdp,ngpc->ngdc', c_im, abi, precision=hp))
    lag = jnp.arange(n)[None, :] - jnp.arange(n)[:, None]
    toep = ktau[jnp.clip(lag, 0, n)]
    toep = jnp.where((lag >= 0)[:, :, None, None, None], toep, 0.0)
    g = lam_re.shape[0]
    toep = toep.transpose(2, 0, 4, 1, 3).reshape(g, n * S5_GROUP, n * S5_GROUP)
    rev = n - 1 - jnp.arange(n)
    b_end = jnp.concatenate([abr[rev], abi[rev]], axis=2)
    b_end = b_end.transpose(1, 0, 3, 2).reshape(g, n * S5_GROUP, 2 * S5_STATE)
    cr = c_re[None] * pr[1:, :, None, :] - c_im[None] * pi[1:, :, None, :]
    ci = c_re[None] * pi[1:, :, None, :] + c_im[None] * pr[1:, :, None, :]
    c_in = jnp.concatenate([cr, -ci], axis=3)
    c_in = c_in.transpose(1, 3, 0, 2).reshape(g, 2 * S5_STATE, n * S5_GROUP)
    rows = []
    for k in range(n_double):
        sr, si = a_pow(float(n * 2 ** k))
        rows += [jnp.concatenate([sr, sr], axis=1), jnp.concatenate([-si, si], axis=1)]
    step = jnp.stack(rows, axis=1)
    return _s5_slab_tables(toep, b_end, c_in, step)


def _s5_slab_tables(toep, b_end, c_in, step):
    n, c, p, q = S5_CHUNK, S5_GROUP, S5_STATE, S5_SLAB
    j = S5_GROUPS // q
    eye = jnp.eye(q, dtype=toep.dtype)
    tt = jnp.einsum('jqsctd,qr->jsqctrd', toep.reshape(j, q, n, c, n, c), eye).reshape(j, n * q * c, n * q * c)
    bb = jnp.einsum('jqscip,qr->jsqcirp', b_end.reshape(j, q, n, c, 2, p), eye).reshape(j, n * q * c, 2 * q * p)
    cc = jnp.einsum('jqiptd,qr->jiqptrd', c_in.reshape(j, q, 2, p, n, c), eye).reshape(j, 2 * q * p, n * q * c)
    nd2 = step.shape[1]
    st = step.reshape(j, q, nd2, 2, p).transpose(0, 2, 3, 1, 4).reshape(j, nd2, 2 * q * p)
    return tt.astype(BF16), bb.astype(BF16), cc.astype(BF16), st


def _s5_kernel(u_ref, tt_ref, bb_ref, cc_ref, st_ref, y_ref, *, n_double):
    n_chunk = u_ref.shape[0]
    half = S5_SLAB * S5_STATE
    u = jnp.concatenate([u_ref[:, s, :].astype(BF16) for s in range(S5_CHUNK)], axis=1)
    x = jnp.dot(u, bb_ref[0], preferred_element_type=F32)
    row = lax.broadcasted_iota(jnp.int32, x.shape, 0)
    for k in range(n_double):
        sh = jnp.where(row >= 2 ** k, pltpu.roll(x, 2 ** k, 0), 0.0)
        x = x + st_ref[0, 2 * k:2 * k + 1, :] * sh + st_ref[0, 2 * k + 1:2 * k + 2, :] * pltpu.roll(sh, half, 1)
    xin = jnp.where(row >= 1, pltpu.roll(x, 1, 0), 0.0).astype(BF16)
    y = jnp.dot(u, tt_ref[0], preferred_element_type=F32) + jnp.dot(xin, cc_ref[0], preferred_element_type=F32)
    lanes = u_ref.shape[2]
    for t in range(S5_CHUNK):
        y_ref[:, t, :] = y[:, t * lanes:(t + 1) * lanes]


def _s5_scan(u, tables, n_batch, seq):
    tt, bb, cc, st = tables
    n_chunk = seq // S5_CHUNK
    n_slab = tt.shape[0]
    lanes = S5_SLAB * S5_GROUP
    u3 = u.reshape(n_batch * n_chunk, S5_CHUNK, S5_WIDTH)
    per_slab = lambda a: pl.BlockSpec((1,) + a.shape[1:], lambda j, b: (j, 0, 0))
    io_spec = pl.BlockSpec((n_chunk, S5_CHUNK, lanes), lambda j, b: (b, 0, j))
    y = pl.pallas_call(
        functools.partial(_s5_kernel, n_double=st.shape[1] // 2),
        grid=(n_slab, n_batch),
        in_specs=[io_spec, per_slab(tt), per_slab(bb), per_slab(cc), per_slab(st)],
        out_specs=io_spec,
        out_shape=jax.ShapeDtypeStruct(u3.shape, F32),
        compiler_params=_cparams(2),
        name="s5_chunk",
    )(u3, tt, bb, cc, st)
    return y.reshape(u.shape)


def _attn_kernel(lam_ref, sg_ref, q_ref, k_ref, vt_ref, o_ref, m_sc, acc_sc, *, tq, lambda_init):
    tk = vt_ref.shape[2]
    qi = pl.program_id(2)
    q = q_ref[...]
    lane = lax.broadcasted_iota(jnp.int32, q.shape, 1)
    zero = jnp.zeros_like(q)
    q2 = jnp.concatenate([jnp.where(lane < DA_HEAD_DIM, q, zero), jnp.where(lane >= DA_HEAD_DIM, q, zero)], axis=0)
    m_sc[...] = jnp.full(m_sc.shape, NEG_BIG, F32)
    acc_sc[...] = jnp.zeros(acc_sc.shape, F32)

    n_strip = 2 * tq // ATTN_STRIP
    n_sub = tq // tk
    ones_rows = jnp.ones((16, tk), BF16)

    def step(j, masked):
        kb = k_ref[pl.ds(pl.multiple_of(j * tq, tq), tq), :]
        vtb = [jnp.concatenate([vt_ref[j * n_sub + i], ones_rows], axis=0) for i in range(n_sub)]
        s = [lax.dot_general(kb, q2[c * ATTN_STRIP:(c + 1) * ATTN_STRIP], (((1,), (1,)), ((), ())),
                             preferred_element_type=F32) for c in range(n_strip)]
        for c in range(n_strip):
            cs = slice(c * ATTN_STRIP, (c + 1) * ATTN_STRIP)
            sc = s[c]
            if masked:
                key = lax.broadcasted_iota(jnp.int32, sc.shape, 0)
                col = lax.broadcasted_iota(jnp.int32, sc.shape, 1)
                sc = jnp.where(key <= col + (c * ATTN_STRIP) % tq, sc, NEG_BIG)
            m_prev = m_sc[:, cs]
            m_next = jnp.maximum(m_prev, jnp.max(sc, axis=0, keepdims=True))
            alpha = jnp.exp2(m_prev - m_next)
            p = jnp.exp2(sc - m_next).astype(BF16)
            m_sc[:, cs] = m_next
            pv = sum(jnp.dot(vtb[i], p[i * tk:(i + 1) * tk], preferred_element_type=F32) for i in range(n_sub))
            acc_sc[:, cs] = alpha * acc_sc[:, cs] + pv

    def body(j, carry):
        step(j, False)
        return carry

    lax.fori_loop(0, qi, body, 0)
    step(qi, True)

    lv = lam_ref[...]
    lam = (jnp.exp(jnp.sum(lv[0:1] * lv[1:2], axis=1, keepdims=True))
           - jnp.exp(jnp.sum(lv[2:3] * lv[3:4], axis=1, keepdims=True)) + lambda_init)
    o_all = acc_sc[:DA_V_DIM, :] / acc_sc[DA_V_DIM:DA_V_DIM + 1, :]
    o = (o_all[:, :tq] - lam * o_all[:, tq:]).T
    o = o * lax.rsqrt(jnp.mean(o * o, axis=-1, keepdims=True) + DA_SUBLN_EPS) * sg_ref[...] * (1.0 - lambda_init)
    o_ref[...] = o.astype(o_ref.dtype)


def _diff_attn(q, k, vt, lam_vecs, subln_g, lambda_init, n_batch, seq):
    tk = vt.shape[2]
    tq = min(ATTN_TQ, seq)
    nq = seq // tq
    q_spec = pl.BlockSpec((tq, DA_V_DIM), lambda b, h, i: (b * nq + i, h))
    k_spec = pl.BlockSpec((seq, DA_V_DIM), lambda b, h, i: (b, h))
    vt_spec = pl.BlockSpec((seq // tk, DA_V_DIM, tk), lambda b, h, i: (b, h, 0))
    return pl.pallas_call(
        functools.partial(_attn_kernel, tq=tq, lambda_init=lambda_init),
        grid=(n_batch, DA_HEADS, nq),
        in_specs=[_const_spec(lam_vecs.shape), _const_spec((1, DA_V_DIM)), q_spec, k_spec, vt_spec],
        out_specs=q_spec,
        out_shape=jax.ShapeDtypeStruct((n_batch * seq, DA_WIDTH), BF16),
        scratch_shapes=[pltpu.VMEM((1, 2 * tq), F32), pltpu.VMEM((DA_V_DIM + 16, 2 * tq), F32)],
        compiler_params=_cparams(3),
        name="diff_attn",
    )(lam_vecs, subln_g[None, :], q, k, vt)


def _rw_prep_kernel(*refs, tm, seq, gated):
    if gated:
        (z_ref, zp_ref, vf_ref, mu_ref, w0_ref, a0_ref, kk_ref, ka_ref, rk_ref, w2h_ref, w2l_ref, a2h_ref, a2l_ref,
         g2h_ref, g2l_ref, e64_ref, tri_ref, last_ref, v0_ref, v1h_ref, v1l_ref, v2h_ref, v2l_ref,
         rh_ref, ah_ref, bh_ref, kh_ref, bt_ref, kt_ref, v_ref, gc_ref, bonus_ref, g_ref) = refs
    else:
        (z_ref, zp_ref, mu_ref, w0_ref, a0_ref, kk_ref, ka_ref, rk_ref, w2h_ref, w2l_ref, a2h_ref, a2l_ref,
         g2h_ref, g2l_ref, e64_ref, tri_ref, last_ref,
         rh_ref, ah_ref, bh_ref, kh_ref, bt_ref, kt_ref, v_ref, gc_ref, bonus_ref, g_ref, vfirst_ref) = refs
    i = pl.program_id(0)
    z = z_ref[...]
    first = jnp.where((i * tm) % seq == 0, 0.0, 1.0) * zp_ref[7:8, :]
    row = lax.broadcasted_iota(jnp.int32, z.shape, 0)
    prev = jnp.where(row == 0, first, pltpu.roll(z, 1, 0))
    zf = z + (prev - z) * mu_ref[...]
    r = zf[:, :RW_WIDTH]
    k = zf[:, RW_WIDTH:2 * RW_WIDTH]
    v = zf[:, 2 * RW_WIDTH:3 * RW_WIDTH]
    tail = zf[:, 3 * RW_WIDTH:]
    w_in = w0_ref[...] + _mm_split(jnp.tanh(tail), w2h_ref[...], w2l_ref[...])
    softplus = jnp.maximum(-w_in, 0.0) + jnp.log(1.0 + jnp.exp(-jnp.abs(w_in)))
    logw = -jnp.exp(-softplus - 0.5)
    a = _sigmoid(a0_ref[...] + _mm_split(tail, a2h_ref[...], a2l_ref[...]))
    g_ref[...] = _mm_split(_sigmoid(tail), g2h_ref[...], g2l_ref[...])
    kk = k * kk_ref[...]
    norm = jnp.sqrt(_mm_sel_right(kk * kk, e64_ref[...]))
    kk = kk / jnp.maximum(norm, 1e-12)
    k = k * (1.0 + (a - 1.0) * ka_ref[...])
    if gated:
        lo = _mm_split(v, v1h_ref[...], v1l_ref[...])
        mix = _sigmoid(v0_ref[...] + _mm_split(lo, v2h_ref[...], v2l_ref[...]))
        v = v + (vf_ref[...] - v) * mix
    else:
        vfirst_ref[...] = v
    bonus_ref[...] = _mm_sel_right(r * k * rk_ref[...], e64_ref[...]) * v
    cs = _mm_sel_left(tri_ref[...], logw)
    cs_end = _mm_sel_left(last_ref[...], cs)
    inv = jnp.exp(-cs)
    to_end = jnp.exp(cs_end - cs)
    b = kk * a
    rh_ref[...] = (r * jnp.exp(cs)).astype(BF16)
    ah_ref[...] = (-kk * jnp.exp(cs - logw)).astype(BF16)
    bh_ref[...] = (b * inv).astype(BF16)
    kh_ref[...] = (k * inv).astype(BF16)
    bt_ref[...] = (b * to_end).astype(BF16)
    kt_ref[...] = (k * to_end).astype(BF16)
    v_ref[...] = v.astype(BF16)
    gc_ref[...] = jnp.exp(cs_end)


def _pad_rows(w, start):
    return jnp.zeros((RW_TAIL, w.shape[1]), w.dtype).at[start:start + w.shape[0]].set(w)


def _hi_lo(w):
    hi = w.astype(BF16)
    return hi, (w - hi.astype(w.dtype)).astype(BF16)


def _rw_prep(z, p, v_first, seq, tm):
    t = z.shape[0]
    gated = v_first is not None
    idx = jnp.arange(tm)
    same_chunk = (idx[:, None] // RW_CHUNK) == (idx[None, :] // RW_CHUNK)
    tri = (same_chunk & (idx[None, :] <= idx[:, None])).astype(BF16)
    last = (same_chunk & (idx[None, :] % RW_CHUNK == RW_CHUNK - 1)).astype(BF16)
    vec = lambda a: a[None, :]
    consts = [vec(p['mu']), vec(p['w0']), vec(p['a0']), vec(p['k_k']), vec(p['k_a']), vec(p['r_k'].reshape(-1)),
              *_hi_lo(_pad_rows(p['w2'], 0)), *_hi_lo(_pad_rows(p['a2'], RW_DECAY_LORA)),
              *_hi_lo(_pad_rows(p['g2'], RW_DECAY_LORA + RW_A_LORA)),
              _group_ones(RW_WIDTH, RW_HEAD), tri, last]
    if gated:
        consts += [vec(p['v0']), *_hi_lo(p['v1']), *_hi_lo(p['v2'])]
    row = lambda width: pl.BlockSpec((tm, width), lambda i: (i, 0))
    zp_spec = pl.BlockSpec((8, RW_SHIFT_WIDTH), lambda i: (jnp.maximum(i * (tm // 8) - 1, 0), 0))
    in_specs = [row(RW_SHIFT_WIDTH), zp_spec] + ([row(RW_WIDTH)] if gated else []) + [_const_spec(c.shape) for c in consts]
    args = [z, z] + ([v_first] if gated else []) + consts
    n_out = 10 if gated else 11
    dtypes = [BF16] * 7 + [F32] * 3 + ([] if gated else [F32])
    outs = pl.pallas_call(
        functools.partial(_rw_prep_kernel, tm=tm, seq=seq, gated=gated),
        grid=(t // tm,),
        in_specs=in_specs,
        out_specs=[row(RW_WIDTH)] * n_out,
        out_shape=[jax.ShapeDtypeStruct((t, RW_WIDTH), dt) for dt in dtypes],
        compiler_params=_cparams(1),
        name="rw_prep",
    )(*args)
    return outs


def _rw_intra_kernel(rh_ref, ah_ref, bh_ref, kh_ref, bt_ref, kt_ref, v_ref, gc_ref,
                     wr_ref, mb_ref, uloc_ref, oloc_ref, kvloc_ref, gcol_ref, *, n_sub):
    n = RW_CHUNK
    ri = lax.broadcasted_iota(jnp.int32, (n, n), 0)
    ci = lax.broadcasted_iota(jnp.int32, (n, n), 1)
    strict = ci < ri
    incl = ci <= ri
    eye = ci == ri
    eye_b = eye.astype(BF16)
    ones_b = jnp.ones((n, n), BF16)
    items = [(c, h) for c in range(n_sub) for h in range(RW_HEADS)]
    rows = lambda c: slice(c * n, (c + 1) * n)
    lanes = lambda h: slice(h * RW_HEAD, (h + 1) * RW_HEAD)
    ld = lambda ref: [ref[rows(c), lanes(h)] for c, h in items]
    rh, ah, bh, kh, bt, kt, v = (ld(r) for r in (rh_ref, ah_ref, bh_ref, kh_ref, bt_ref, kt_ref, v_ref))
    ar = [jnp.concatenate([a, r], axis=0) for a, r in zip(ah, rh)]
    gb = [_mm_nt(x, y) for x, y in zip(ar, bh)]
    gk = [_mm_nt(x, y) for x, y in zip(ar, kh)]
    l_ab = [jnp.where(strict, g[:n], 0.0) for g in gb]
    m_rb =[jnp.where(incl, g[n:], 0.0).astype(BF16) for g in gb]
    lm_k = [jnp.concatenate([jnp.where(strict, g[:n], 0.0), jnp.where(incl, g[n:], 0.0)], axis=0).astype(BF16) for g in gk]
    akv = [_mm(x, y) for x, y in zip(lm_k, v)]
    tinv = [jnp.where(eye, 1.0, x) for x in l_ab]
    l_ab = [x.astype(BF16) for x in l_ab]
    pw = [_mm(x, x) for x in l_ab]
    for _ in range(4):
        pw_b = [x.astype(BF16) for x in pw]
        res = [_mm(jnp.concatenate([p, t.astype(BF16)], axis=0), p) for p, t in zip(pw_b, tinv)]
        pw = [r[:n] for r in res]
        tinv = [t + r[n:] for t, r in zip(tinv, res)]
    tinv = [(t + _mm(t, p)).astype(BF16) for t, p in zip(tinv, pw)]
    w = [_mm(t, a) for t, a in zip(tinv, ah)]
    uloc = [_mm(t, x[:n]) for t, x in zip(tinv, akv)]
    kvloc = [_mm_tn(x, y) for x, y in zip(kt, v)]
    bt_t = [_mm_tn(x, eye_b) for x in bt]
    for i, (c, h) in enumerate(items):
        decay = jnp.where(eye, jnp.broadcast_to(gc_ref[(c + 1) * n - 1:(c + 1) * n, lanes(h)], (n, n)), 0.0)
        gcol_ref[rows(c), lanes(h)] = _mm_sel_right(decay, ones_b)
        wr_ref[2 * c * n:(2 * c + 1) * n, lanes(h)] = w[i].astype(BF16)
        wr_ref[(2 * c + 1) * n:(2 * c + 2) * n, lanes(h)] = rh[i]
        mb_ref[2 * c * n:(2 * c + 1) * n, lanes(h)] = m_rb[i]
        mb_ref[(2 * c + 1) * n:(2 * c + 2) * n, lanes(h)] = bt_t[i].astype(BF16)
        uloc_ref[rows(c), lanes(h)] = uloc[i]
        oloc_ref[rows(c), lanes(h)] = akv[i][n:]
        kvloc_ref[rows(c), lanes(h)] = kvloc[i]


def _rw_seq_kernel(wr_ref, mb_ref, uloc_ref, oloc_ref, kvloc_ref, gcol_ref, y_ref, h_sc, *, n_batch):
    @pl.when(pl.program_id(0) == 0)
    def _():
        h_sc[...] = jnp.zeros(h_sc.shape, F32)

    n = RW_CHUNK
    items = [(b, h) for b in range(n_batch) for h in range(RW_HEADS)]
    lanes = lambda h: slice(h * RW_HEAD, (h + 1) * RW_HEAD)
    h0 = [h_sc[i] for i in range(len(items))]
    s1 = [_mm(wr_ref[b, :, lanes(h)], x) for (b, h), x in zip(items, h0)]
    u = [x[:n] + uloc_ref[b, :, lanes(h)] for (b, h), x in zip(items, s1)]
    s2 = [_mm(mb_ref[b, :, lanes(h)], x) for (b, h), x in zip(items, u)]
    for i, (b, h) in enumerate(items):
        y_ref[b, :, lanes(h)] = s1[i][n:] + s2[i][:n] + oloc_ref[b, :, lanes(h)]
        h_sc[i] = gcol_ref[b, :, lanes(h)] * h0[i] + s2[i][n:] + kvloc_ref[b, :, lanes(h)]


def _rw_scan(ops, n_batch, seq):
    t = n_batch * seq
    n_sub = 4
    rows = n_sub * RW_CHUNK
    spec = pl.BlockSpec((rows, RW_WIDTH), lambda i: (i, 0))
    spec2 = pl.BlockSpec((2 * rows, RW_WIDTH), lambda i: (i, 0))
    wr, mb, uloc, oloc, kvloc, gcol = pl.pallas_call(
        functools.partial(_rw_intra_kernel, n_sub=n_sub),
        grid=(t // rows,),
        in_specs=[spec] * 8,
        out_specs=[spec2, spec2, spec, spec, spec, spec],
        out_shape=[jax.ShapeDtypeStruct((2 * t, RW_WIDTH), BF16)] * 2 + [jax.ShapeDtypeStruct((t, RW_WIDTH), F32)] * 4,
        compiler_params=_cparams(1),
        name="rw_intra",
    )(*ops)
    n_chunk = seq // RW_CHUNK
    b3 = lambda a: a.reshape(n_batch, -1, RW_WIDTH)
    cspec = pl.BlockSpec((n_batch, RW_CHUNK, RW_WIDTH), lambda c: (0, c, 0))
    cspec2 = pl.BlockSpec((n_batch, 2 * RW_CHUNK, RW_WIDTH), lambda c: (0, c, 0))
    y = pl.pallas_call(
        functools.partial(_rw_seq_kernel, n_batch=n_batch),
        grid=(n_chunk,),
        in_specs=[cspec2, cspec2, cspec, cspec, cspec, cspec],
        out_specs=cspec,
        out_shape=jax.ShapeDtypeStruct((n_batch, seq, RW_WIDTH), F32),
        scratch_shapes=[pltpu.VMEM((n_batch * RW_HEADS, RW_HEAD, RW_HEAD), F32)],
        compiler_params=_cparams(1),
        name="rw_seq",
    )(b3(wr), b3(mb), b3(uloc), b3(oloc), b3(kvloc), b3(gcol))
    return y.reshape(t, RW_WIDTH)


def _merge_kernel(x_ref, ys5_ref, us5_ref, yb_ref, yrw_ref, bonus_ref, g_ref, gl_ref, d_ref, wglu_ref, lng_ref, lnb_ref,
                  e64_ref, wb_ref, wout_ref, o_ref):
    ya = ys5_ref[...] + d_ref[...] * us5_ref[...]
    ya = 0.5 * ya * (1.0 + jnp.tanh(math.sqrt(2.0 / math.pi) * (ya + 0.044715 * (ya * ya * ya))))
    ya = ya * _sigmoid(_mm(ya, wglu_ref[...]))
    y = yrw_ref[...]
    mean = _mm_sel_right(y, e64_ref[...]) * (1.0 / RW_HEAD)
    yc = y - mean
    var = _mm_sel_right(yc * yc, e64_ref[...]) * (1.0 / RW_HEAD)
    yc = yc * lax.rsqrt(var + RW_LN_EPS) * lng_ref[...] + lnb_ref[...]
    yc = (yc + bonus_ref[...]) * g_ref[...]
    merged = jnp.zeros((x_ref.shape[0], D_MODEL), F32)
    for n, br in enumerate((ya, yb_ref[...], yc)):
        gate = _sigmoid(gl_ref[:, n * D_MODEL:(n + 1) * D_MODEL])
        merged = merged + gate * _mm(br, wb_ref[n])
    o_ref[...] = x_ref[...] + _mm(merged, wout_ref[...])


def _merge(x2, ys5, us5, yb, yrw, bonus, g, gl, s5_d, w_glu, ln_g, ln_b, w_branch, w_out, tm):
    t = x2.shape[0]
    consts = [s5_d[None, :], w_glu.astype(BF16), ln_g[None, :], ln_b[None, :], _group_ones(RW_WIDTH, RW_HEAD),
              w_branch.astype(BF16), w_out.astype(BF16)]
    row = lambda width: pl.BlockSpec((tm, width), lambda i: (i, 0))
    widths = [D_MODEL, S5_WIDTH, S5_WIDTH, DA_WIDTH, RW_WIDTH, RW_WIDTH, RW_WIDTH, N_BRANCH * D_MODEL]
    return pl.pallas_call(
        _merge_kernel,
        grid=(t // tm,),
        in_specs=[row(wd) for wd in widths] + [_const_spec(c.shape) for c in consts],
        out_specs=row(D_MODEL),
        out_shape=jax.ShapeDtypeStruct((t, D_MODEL), F32),
        compiler_params=_cparams(1),
        name="merge",
    )(x2, ys5, us5, yb, yrw, bonus, g, gl, *consts)


def _ffn_kernel(x_ref, g_ref, win_ref, wout_ref, o_ref):
    x = x_ref[...]
    h = (x * lax.rsqrt(jnp.mean(x * x, axis=-1, keepdims=True) + NORM_EPS) * g_ref[...]).astype(BF16)
    gate = jnp.dot(h, win_ref[:, :D_FF], preferred_element_type=F32)
    up = jnp.dot(h, win_ref[:, D_FF:], preferred_element_type=F32)
    act = gate * _sigmoid(gate) * up
    o_ref[...] = x + _mm(act, wout_ref[...])


def _ffn(x2, g, w_in, w_out, tm):
    t = x2.shape[0]
    consts = [g[None, :], w_in.astype(BF16), w_out.astype(BF16)]
    row = pl.BlockSpec((tm, D_MODEL), lambda i: (i, 0))
    return pl.pallas_call(
        _ffn_kernel,
        grid=(t // tm,),
        in_specs=[row] + [_const_spec(c.shape) for c in consts],
        out_specs=row,
        out_shape=jax.ShapeDtypeStruct((t, D_MODEL), F32),
        compiler_params=_cparams(1),
        name="ffn",
    )(x2, *consts)


def _layer(x2, i, n_batch, seq, p, v_first, tm):
    us5, q, k, v, zrw, gl = _in_proj(x2, p['norm1_g'], p['w_in'], p['da_q_gain'], p['da_k_gain'], tm)
    n_double = max(1, (seq // S5_CHUNK - 1).bit_length())
    tables = _s5_tables(p['s5_lambda_re'], p['s5_lambda_im'], p['s5_log_dt'], p['s5_b_re'], p['s5_b_im'],
                        p['s5_c_re'], p['s5_c_im'], n_double)
    ys5 = _s5_scan(us5, tables, n_batch, seq)
    lambda_init = 0.8 - 0.6 * math.exp(-0.3 * i)
    yb = _diff_attn(q, k, v, p['da_lambda'], p['da_subln_g'], lambda_init, n_batch, seq)
    rw = {name[3:]: val for name, val in p.items() if name.startswith('rw_')}
    outs = _rw_prep(zrw, rw, v_first, seq, tm)
    if v_first is None:
        v_first = outs[10]
    yrw = _rw_scan(outs[:8], n_batch, seq)
    x2 = _merge(x2, ys5, us5, yb, yrw, outs[8], outs[9], gl, p['s5_d'], p['s5_w_glu'], p['rw_ln_g'], p['rw_ln_b'],
                p['w_branch'], p['w_out'], tm)
    x2 = _ffn(x2, p['norm2_g'], p['w_ffn_in'], p['w_ffn_out'], tm)
    return x2, v_first


def kernel(x, norm1_g, w_in, s5_lambda_re, s5_lambda_im, s5_log_dt, s5_b_re, s5_b_im, s5_c_re, s5_c_im, s5_d, s5_w_glu, da_q_gain, da_k_gain, da_lambda, da_subln_g, rw_mu, rw_w0, rw_w2, rw_a0, rw_a2, rw_g2, rw_k_k, rw_k_a, rw_r_k, rw_ln_g, rw_ln_b, rw_v0, rw_v1, rw_v2, w_branch, w_out, norm2_g, w_ffn_in, w_ffn_out):
    n_batch, seq, _ = x.shape
    per_layer = dict(norm1_g=norm1_g, w_in=w_in, s5_lambda_re=s5_lambda_re, s5_lambda_im=s5_lambda_im,
                     s5_log_dt=s5_log_dt, s5_b_re=s5_b_re, s5_b_im=s5_b_im, s5_c_re=s5_c_re, s5_c_im=s5_c_im,
                     s5_d=s5_d, s5_w_glu=s5_w_glu, da_q_gain=da_q_gain, da_k_gain=da_k_gain, da_lambda=da_lambda,
                     da_subln_g=da_subln_g, rw_mu=rw_mu, rw_w0=rw_w0, rw_w2=rw_w2, rw_a0=rw_a0, rw_a2=rw_a2,
                     rw_g2=rw_g2, rw_k_k=rw_k_k, rw_k_a=rw_k_a, rw_r_k=rw_r_k, rw_ln_g=rw_ln_g, rw_ln_b=rw_ln_b,
                     w_branch=w_branch, w_out=w_out, norm2_g=norm2_g, w_ffn_in=w_ffn_in, w_ffn_out=w_ffn_out)
    tm = min(256, seq)
    x2 = x.reshape(n_batch * seq, D_MODEL)
    v_first = None
    for i in range(w_in.shape[0]):
        p = {name: val[i] for name, val in per_layer.items()}
        if i > 0:
            p.update(rw_v0=rw_v0[i - 1], rw_v1=rw_v1[i - 1], rw_v2=rw_v2[i - 1])
        x2, v_first = _layer(x2, i, n_batch, seq, p, v_first, tm)
    return x2.reshape(x.shape)
```

```python
import functools
import math

import jax
import jax.numpy as jnp
from jax import lax
from jax.experimental import pallas as pl
from jax.experimental.pallas import tpu as pltpu

F32 = jnp.float32
BF16 = jnp.bfloat16

D_MODEL = 1024
DEPTH = 2
S5_WIDTH = 512
S5_GROUP = 16
S5_GROUPS = S5_WIDTH // S5_GROUP
S5_STATE = 64
S5_CHUNK = 16
S5_SLAB = 8
DA_HEADS = 4
DA_HEAD_DIM = 64
DA_V_DIM = 2 * DA_HEAD_DIM
DA_WIDTH = DA_HEADS * DA_V_DIM
DA_SUBLN_EPS = 1e-5
RW_HEAD = 64
RW_WIDTH = 512
RW_HEADS = RW_WIDTH // RW_HEAD
RW_DECAY_LORA = 32
RW_A_LORA = 32
RW_G_LORA = 96
RW_TAIL = RW_DECAY_LORA + RW_A_LORA + RW_G_LORA
RW_SHIFT_WIDTH = 3 * RW_WIDTH + RW_TAIL
RW_LN_EPS = 64e-5
RW_CHUNK = 64
N_BRANCH = 3
D_FF = 2816
NORM_EPS = 1e-6
NEG_BIG = -1e30
ATTN_TQ = 512
ATTN_STRIP = 256

VMEM_LIMIT = 56 * 1024 * 1024


def _cparams(n_axes):
    return pltpu.CompilerParams(dimension_semantics=("arbitrary",) * n_axes, vmem_limit_bytes=VMEM_LIMIT)


def _const_spec(shape):
    nd = len(shape)
    return pl.BlockSpec(shape, lambda *_: (0,) * nd, pipeline_mode=pl.Buffered(1))


def _mm(a, b):
    return jnp.dot(a.astype(BF16), b.astype(BF16), preferred_element_type=F32)


def _mm_nt(a, b):
    return lax.dot_general(a.astype(BF16), b.astype(BF16), (((1,), (1,)), ((), ())), preferred_element_type=F32)


def _mm_tn(a, b):
    return lax.dot_general(a.astype(BF16), b.astype(BF16), (((0,), (0,)), ((), ())), preferred_element_type=F32)


def _split2(x):
    hi = x.astype(BF16)
    lo = (x - hi.astype(F32)).astype(BF16)
    return hi, lo


def _split3(x):
    hi = x.astype(BF16)
    r = x - hi.astype(F32)
    mid = r.astype(BF16)
    lo = (r - mid.astype(F32)).astype(BF16)
    return hi, mid, lo


def _mm_sel_right(x, sel):
    hi, mid, lo = _split3(x)
    d = lambda p: jnp.dot(p, sel, preferred_element_type=F32)
    return d(hi) + d(mid) + d(lo)


def _mm_sel_left(sel, x):
    hi, mid, lo = _split3(x)
    d = lambda p: jnp.dot(sel, p, preferred_element_type=F32)
    return d(hi) + d(mid) + d(lo)


def _mm_split(x, w_hi, w_lo):
    x_hi, x_lo = _split2(x)
    d = lambda p, q: jnp.dot(p, q, preferred_element_type=F32)
    return d(x_hi, w_hi) + d(x_hi, w_lo) + d(x_lo, w_hi)


def _sigmoid(x):
    return 1.0 / (1.0 + jnp.exp(-x))


def _group_ones(width, group):
    r = jnp.arange(width) // group
    return (r[:, None] == r[None, :]).astype(BF16)


def _in_proj_kernel(x_ref, g_ref, ws5_ref, wq_ref, wk_ref, wvt_ref, wrw_ref, wgate_ref, e64_ref, qg_ref, kg_ref,
                    us5_ref, q_ref, k_ref, vt_ref, zrw_ref, gl_ref):
    x = x_ref[...]
    h = x * lax.rsqrt(jnp.mean(x * x, axis=-1, keepdims=True) + NORM_EPS) * g_ref[...]
    h = h.astype(BF16)
    dot = lambda w_ref: jnp.dot(h, w_ref[...], preferred_element_type=F32)
    us5_ref[...] = dot(ws5_ref)

    def qk_norm(t, gain_ref):
        ms = _mm_sel_right(t * t, e64_ref[...]) * (1.0 / DA_HEAD_DIM)
        return t * lax.rsqrt(ms + NORM_EPS) * gain_ref[...]

    q_ref[...] = qk_norm(dot(wq_ref), qg_ref).astype(BF16)
    k_ref[...] = qk_norm(dot(wk_ref), kg_ref).astype(BF16)
    vt_ref[0] = lax.dot_general(wvt_ref[...], h, (((1,), (1,)), ((), ())), preferred_element_type=F32).astype(BF16)
    zrw_ref[...] = dot(wrw_ref)
    gl_ref[...] = dot(wgate_ref)


def _in_proj(x2, g, w, q_gain, k_gain, tm):
    t = x2.shape[0]
    w = w.astype(BF16)
    o = 0
    parts = []
    for width in (S5_WIDTH, DA_WIDTH, DA_WIDTH, DA_WIDTH, RW_SHIFT_WIDTH, N_BRANCH * D_MODEL):
        parts.append(w[:, o:o + width])
        o += width
    parts[3] = parts[3].T
    e64 = _group_ones(DA_WIDTH, DA_HEAD_DIM)
    qg = jnp.tile(q_gain.reshape(-1), DA_HEADS)[None, :] * (DA_HEAD_DIM ** -0.5 * math.log2(math.e))
    kg = jnp.tile(k_gain.reshape(-1), DA_HEADS)[None, :]
    row = lambda width: pl.BlockSpec((tm, width), lambda i: (i, 0))
    consts = [g[None, :]] + parts + [e64, qg, kg]
    out_specs = [row(S5_WIDTH), row(DA_WIDTH), row(DA_WIDTH), pl.BlockSpec((1, DA_WIDTH, tm), lambda i: (i, 0, 0)),
                 row(RW_SHIFT_WIDTH), row(N_BRANCH * D_MODEL)]
    out_shape = [jax.ShapeDtypeStruct((t, S5_WIDTH), F32), jax.ShapeDtypeStruct((t, DA_WIDTH), BF16),
                 jax.ShapeDtypeStruct((t, DA_WIDTH), BF16), jax.ShapeDtypeStruct((t // tm, DA_WIDTH, tm), BF16),
                 jax.ShapeDtypeStruct((t, RW_SHIFT_WIDTH), F32), jax.ShapeDtypeStruct((t, N_BRANCH * D_MODEL), F32)]
    return pl.pallas_call(
        _in_proj_kernel,
        grid=(t // tm,),
        in_specs=[row(D_MODEL)] + [_const_spec(c.shape) for c in consts],
        out_specs=out_specs,
        out_shape=out_shape,
        compiler_params=_cparams(1),
        name="in_proj",
    )(x2, *consts)


def _s5_tables(lam_re, lam_im, log_dt, b_re, b_im, c_re, c_im, n_double):
    n = S5_CHUNK
    dt = jnp.exp(log_dt)[:, None]
    den = lam_re * lam_re + lam_im * lam_im

    def a_pow(tau):
        mag = jnp.exp(lam_re * dt * tau)
        return mag * jnp.cos(lam_im * dt * tau), mag * jnp.sin(lam_im * dt * tau)

    ar, ai = a_pow(1.0)
    nr = ar - 1.0
    kr = (nr * lam_re + ai * lam_im) / den
    ki = (ai * lam_re - nr * lam_im) / den
    bbar_r = kr[..., None] * b_re - ki[..., None] * b_im
    bbar_i = kr[..., None] * b_im + ki[..., None] * b_re
    taus = jnp.arange(n + 1, dtype=lam_re.dtype)
    pr, pi = jax.vmap(a_pow)(taus)
    hp = lax.Precision.HIGHEST
    abr = pr[..., None] * bbar_r[None] - pi[..., None] * bbar_i[None]
    abi = pr[..., None] * bbar_i[None] + pi[..., None] * bbar_r[None]
    ktau = (jnp.einsum('gdp,ngpc->ngdc', c_re, abr, precision=hp)
            - jnp.einsum('gdp,ngpc->ngdc', c_im, abi, precision=hp))
    g, q, c, p2 = lam_re.shape[0], S5_SLAB, S5_GROUP, 2 * S5_STATE
    j = g // q
    eye = jnp.eye(q, dtype=ktau.dtype)
    kl = ktau[:n].transpose(1, 0, 3, 2).reshape(j, q, n, c, c).transpose(0, 2, 1, 3, 4)
    klag = (kl[:, :, :, :, None, :] * eye[None, None, :, None, :, None]).reshape(j, n, q * c, q * c)
    rev = n - 1 - jnp.arange(n)
    b_end = jnp.concatenate([abr[rev], abi[rev]], axis=2)
    bend = b_end.reshape(n, j, q, p2, c).transpose(1, 0, 2, 4, 3).reshape(j, n, q * c, p2)
    cr = c_re[None] * pr[1:, :, None, :] - c_im[None] * pi[1:, :, None, :]
    ci = c_re[None] * pi[1:, :, None, :] + c_im[None] * pr[1:, :, None, :]
    c_in = jnp.concatenate([cr, -ci], axis=3)
    cin = c_in.reshape(n, j, q, c, p2).transpose(1, 0, 4, 2, 3).reshape(j, n, p2, q * c)
    rows = []
    for k in range(n_double):
        sr, si = a_pow(float(n * 2 ** k))
        rows += [jnp.concatenate([sr, sr], axis=1), jnp.concatenate([-si, si], axis=1)]
    step = jnp.stack(rows, axis=1)
    step = step.reshape(j, q, 2 * n_double, p2).transpose(0, 2, 1, 3).reshape(j, 2 * n_double, q * p2)
    return klag.astype(BF16), bend.astype(BF16), cin.astype(BF16), step


def _s5_kernel(u_ref, klag_ref, bend_ref, cin_ref, st_ref, y_ref, tt_sc, bb_sc, cc_sc, *, n_double):
    n, w = S5_CHUNK, S5_SLAB * S5_GROUP

    @pl.when(pl.program_id(1) == 0)
    def _():
        zero = jnp.zeros((w, w), BF16)
        row_q = lax.broadcasted_iota(jnp.int32, (w, w), 0) // S5_GROUP
        lane_q = lax.broadcasted_iota(jnp.int32, (w, w), 1) // S5_GROUP
        for s in range(n):
            for t in range(n):
                tt_sc[s * w:(s + 1) * w, t * w:(t + 1) * w] = klag_ref[0, t - s] if t >= s else zero
            b_blk, c_blk = bend_ref[0, s], cin_ref[0, s]
            for q in range(S5_SLAB):
                bb_sc[s * w:(s + 1) * w, q * w:(q + 1) * w] = jnp.where(row_q == q, b_blk, zero)
                cc_sc[q * w:(q + 1) * w, s * w:(s + 1) * w] = jnp.where(lane_q == q, c_blk, zero)

    u = jnp.concatenate([u_ref[:, s, :].astype(BF16) for s in range(n)], axis=1)
    x = jnp.dot(u, bb_sc[...], preferred_element_type=F32)
    row = lax.broadcasted_iota(jnp.int32, x.shape, 0)
    is_re = lax.broadcasted_iota(jnp.int32, x.shape, 1) % (2 * S5_STATE) < S5_STATE
    width = x.shape[1]
    for k in range(n_double):
        sh = jnp.where(row >= 2 ** k, pltpu.roll(x, 2 ** k, 0), 0.0)
        swapped = jnp.where(is_re, pltpu.roll(sh, width - S5_STATE, 1), pltpu.roll(sh, S5_STATE, 1))
        x = x + st_ref[0, 2 * k:2 * k + 1, :] * sh + st_ref[0, 2 * k + 1:2 * k + 2, :] * swapped
    xin = jnp.where(row >= 1, pltpu.roll(x, 1, 0), 0.0).astype(BF16)
    y = jnp.dot(u, tt_sc[...], preferred_element_type=F32) + jnp.dot(xin, cc_sc[...], preferred_element_type=F32)
    for t in range(n):
        y_ref[:, t, :] = y[:, t * w:(t + 1) * w]


def _s5_scan(u, tables, n_batch, seq):
    klag, bend, cin, st = tables
    n_chunk = seq // S5_CHUNK
    n_slab = klag.shape[0]
    w = S5_SLAB * S5_GROUP
    u3 = u.reshape(n_batch * n_chunk, S5_CHUNK, S5_WIDTH)
    per_slab = lambda a: pl.BlockSpec((1,) + a.shape[1:], lambda j, b: (j,) + (0,) * (a.ndim - 1))
    io_spec = pl.BlockSpec((n_chunk, S5_CHUNK, w), lambda j, b: (b, 0, j))
    y = pl.pallas_call(
        functools.partial(_s5_kernel, n_double=st.shape[1] // 2),
        grid=(n_slab, n_batch),
        in_specs=[io_spec, per_slab(klag), per_slab(bend), per_slab(cin), per_slab(st)],
        out_specs=io_spec,
        out_shape=jax.ShapeDtypeStruct(u3.shape, F32),
        scratch_shapes=[pltpu.VMEM((S5_CHUNK * w, S5_CHUNK * w), BF16),
                        pltpu.VMEM((S5_CHUNK * w, S5_SLAB * 2 * S5_STATE), BF16),
                        pltpu.VMEM((S5_SLAB * 2 * S5_STATE, S5_CHUNK * w), BF16)],
        compiler_params=_cparams(2),
        name="s5_chunk",
    )(u3, klag, bend, cin, st)
    return y.reshape(u.shape)


def _attn_kernel(lam_ref, sg_ref, q_ref, k_ref, vt_ref, o_ref, m_sc, acc_sc, *, tq, lambda_init):
    tk = vt_ref.shape[2]
    qi = pl.program_id(2)
    q = q_ref[...]
    lane = lax.broadcasted_iota(jnp.int32, q.shape, 1)
    zero = jnp.zeros_like(q)
    q2 = jnp.concatenate([jnp.where(lane < DA_HEAD_DIM, q, zero), jnp.where(lane >= DA_HEAD_DIM, q, zero)], axis=0)
    m_sc[...] = jnp.full(m_sc.shape, NEG_BIG, F32)
    acc_sc[...] = jnp.zeros(acc_sc.shape, F32)

    n_strip = 2 * tq // ATTN_STRIP
    n_sub = tq // tk
    ones_rows = jnp.ones((16, tk), BF16)

    def step(j, masked):
        kb = k_ref[pl.ds(pl.multiple_of(j * tq, tq), tq), :]
        vtb = [jnp.concatenate([vt_ref[j * n_sub + i], ones_rows], axis=0) for i in range(n_sub)]
        s = [lax.dot_general(kb, q2[c * ATTN_STRIP:(c + 1) * ATTN_STRIP], (((1,), (1,)), ((), ())),
                             preferred_element_type=F32) for c in range(n_strip)]
        for c in range(n_strip):
            cs = slice(c * ATTN_STRIP, (c + 1) * ATTN_STRIP)
            sc = s[c]
            if masked:
                key = lax.broadcasted_iota(jnp.int32, sc.shape, 0)
                col = lax.broadcasted_iota(jnp.int32, sc.shape, 1)
                sc = jnp.where(key <= col + (c * ATTN_STRIP) % tq, sc, NEG_BIG)
            m_prev = m_sc[:, cs]
            m_next = jnp.maximum(m_prev, jnp.max(sc, axis=0, keepdims=True))
            alpha = jnp.exp2(m_prev - m_next)
            p = jnp.exp2(sc - m_next).astype(BF16)
            m_sc[:, cs] = m_next
            pv = sum(jnp.dot(vtb[i], p[i * tk:(i + 1) * tk], preferred_element_type=F32) for i in range(n_sub))
            acc_sc[:, cs] = alpha * acc_sc[:, cs] + pv

    def body(j, carry):
        step(j, False)
        return carry

    lax.fori_loop(0, qi, body, 0)
    step(qi, True)

    lv = lam_ref[...]
    lam = (jnp.exp(jnp.sum(lv[0:1] * lv[1:2], axis=1, keepdims=True))
           - jnp.exp(jnp.sum(lv[2:3] * lv[3:4], axis=1, keepdims=True)) + lambda_init)
    o_all = acc_sc[:DA_V_DIM, :] / acc_sc[DA_V_DIM:DA_V_DIM + 1, :]
    o = (o_all[:, :tq] - lam * o_all[:, tq:]).T
    o = o * lax.rsqrt(jnp.mean(o * o, axis=-1, keepdims=True) + DA_SUBLN_EPS) * sg_ref[...] * (1.0 - lambda_init)
    o_ref[...] = o.astype(o_ref.dtype)


def _diff_attn(q, k, vt, lam_vecs, subln_g, lambda_init, n_batch, seq):
    tk = vt.shape[2]
    tq = min(ATTN_TQ, seq)
    nq = seq // tq
    q_spec = pl.BlockSpec((tq, DA_V_DIM), lambda b, h, i: (b * nq + i, h))
    k_spec = pl.BlockSpec((seq, DA_V_DIM), lambda b, h, i: (b, h))
    vt_spec = pl.BlockSpec((seq // tk, DA_V_DIM, tk), lambda b, h, i: (b, h, 0))
    return pl.pallas_call(
        functools.partial(_attn_kernel, tq=tq, lambda_init=lambda_init),
        grid=(n_batch, DA_HEADS, nq),
        in_specs=[_const_spec(lam_vecs.shape), _const_spec((1, DA_V_DIM)), q_spec, k_spec, vt_spec],
        out_specs=q_spec,
        out_shape=jax.ShapeDtypeStruct((n_batch * seq, DA_WIDTH), BF16),
        scratch_shapes=[pltpu.VMEM((1, 2 * tq), F32), pltpu.VMEM((DA_V_DIM + 16, 2 * tq), F32)],
        compiler_params=_cparams(3),
        name="diff_attn",
    )(lam_vecs, subln_g[None, :], q, k, vt)


def _rw_prep_kernel(*refs, tm, seq, gated):
    if gated:
        (z_ref, zp_ref, vf_ref, mu_ref, w0_ref, a0_ref, kk_ref, ka_ref, rk_ref, w2h_ref, w2l_ref, a2h_ref, a2l_ref,
         g2h_ref, g2l_ref, e64_ref, tri_ref, last_ref, v0_ref, v1h_ref, v1l_ref, v2h_ref, v2l_ref,
         rh_ref, ah_ref, bh_ref, kh_ref, bt_ref, kt_ref, v_ref, gc_ref, bonus_ref, g_ref) = refs
    else:
        (z_ref, zp_ref, mu_ref, w0_ref, a0_ref, kk_ref, ka_ref, rk_ref, w2h_ref, w2l_ref, a2h_ref, a2l_ref,
         g2h_ref, g2l_ref, e64_ref, tri_ref, last_ref,
         rh_ref, ah_ref, bh_ref, kh_ref, bt_ref, kt_ref, v_ref, gc_ref, bonus_ref, g_ref, vfirst_ref) = refs
    i = pl.program_id(0)
    z = z_ref[...]
    first = jnp.where((i * tm) % seq == 0, 0.0, 1.0) * zp_ref[7:8, :]
    row = lax.broadcasted_iota(jnp.int32, z.shape, 0)
    prev = jnp.where(row == 0, first, pltpu.roll(z, 1, 0))
    zf = z + (prev - z) * mu_ref[...]
    r = zf[:, :RW_WIDTH]
    k = zf[:, RW_WIDTH:2 * RW_WIDTH]
    v = zf[:, 2 * RW_WIDTH:3 * RW_WIDTH]
    tail = zf[:, 3 * RW_WIDTH:]
    w_in = w0_ref[...] + _mm_split(jnp.tanh(tail), w2h_ref[...], w2l_ref[...])
    softplus = jnp.maximum(-w_in, 0.0) + jnp.log(1.0 + jnp.exp(-jnp.abs(w_in)))
    logw = -jnp.exp(-softplus - 0.5)
    a = _sigmoid(a0_ref[...] + _mm_split(tail, a2h_ref[...], a2l_ref[...]))
    g_ref[...] = _mm_split(_sigmoid(tail), g2h_ref[...], g2l_ref[...])
    kk = k * kk_ref[...]
    norm = jnp.sqrt(_mm_sel_right(kk * kk, e64_ref[...]))
    kk = kk / jnp.maximum(norm, 1e-12)
    k = k * (1.0 + (a - 1.0) * ka_ref[...])
    if gated:
        lo = _mm_split(v, v1h_ref[...], v1l_ref[...])
        mix = _sigmoid(v0_ref[...] + _mm_split(lo, v2h_ref[...], v2l_ref[...]))
        v = v + (vf_ref[...] - v) * mix
    else:
        vfirst_ref[...] = v
    bonus_ref[...] = _mm_sel_right(r * k * rk_ref[...], e64_ref[...]) * v
    cs = _mm_sel_left(tri_ref[...], logw)
    cs_end = _mm_sel_left(last_ref[...], cs)
    inv = jnp.exp(-cs)
    to_end = jnp.exp(cs_end - cs)
    b = kk * a
    rh_ref[...] = (r * jnp.exp(cs)).astype(BF16)
    ah_ref[...] = (-kk * jnp.exp(cs - logw)).astype(BF16)
    bh_ref[...] = (b * inv).astype(BF16)
    kh_ref[...] = (k * inv).astype(BF16)
    bt_ref[...] = (b * to_end).astype(BF16)
    kt_ref[...] = (k * to_end).astype(BF16)
    v_ref[...] = v.astype(BF16)
    gc_ref[...] = jnp.exp(cs_end)


def _pad_rows(w, start):
    return jnp.zeros((RW_TAIL, w.shape[1]), w.dtype).at[start:start + w.shape[0]].set(w)


def _hi_lo(w):
    hi = w.astype(BF16)
    return hi, (w - hi.astype(w.dtype)).astype(BF16)


def _rw_prep(z, p, v_first, seq, tm):
    t = z.shape[0]
    gated = v_first is not None
    idx = jnp.arange(tm)
    same_chunk = (idx[:, None] // RW_CHUNK) == (idx[None, :] // RW_CHUNK)
    tri = (same_chunk & (idx[None, :] <= idx[:, None])).astype(BF16)
    last = (same_chunk & (idx[None, :] % RW_CHUNK == RW_CHUNK - 1)).astype(BF16)
    vec = lambda a: a[None, :]
    consts = [vec(p['mu']), vec(p['w0']), vec(p['a0']), vec(p['k_k']), vec(p['k_a']), vec(p['r_k'].reshape(-1)),
              *_hi_lo(_pad_rows(p['w2'], 0)), *_hi_lo(_pad_rows(p['a2'], RW_DECAY_LORA)),
              *_hi_lo(_pad_rows(p['g2'], RW_DECAY_LORA + RW_A_LORA)),
              _group_ones(RW_WIDTH, RW_HEAD), tri, last]
    if gated:
        consts += [vec(p['v0']), *_hi_lo(p['v1']), *_hi_lo(p['v2'])]
    row = lambda width: pl.BlockSpec((tm, width), lambda i: (i, 0))
    zp_spec = pl.BlockSpec((8, RW_SHIFT_WIDTH), lambda i: (jnp.maximum(i * (tm // 8) - 1, 0), 0))
    in_specs = [row(RW_SHIFT_WIDTH), zp_spec] + ([row(RW_WIDTH)] if gated else []) + [_const_spec(c.shape) for c in consts]
    args = [z, z] + ([v_first] if gated else []) + consts
    n_out = 10 if gated else 11
    dtypes = [BF16] * 7 + [F32] * 3 + ([] if gated else [F32])
    outs = pl.pallas_call(
        functools.partial(_rw_prep_kernel, tm=tm, seq=seq, gated=gated),
        grid=(t // tm,),
        in_specs=in_specs,
        out_specs=[row(RW_WIDTH)] * n_out,
        out_shape=[jax.ShapeDtypeStruct((t, RW_WIDTH), dt) for dt in dtypes],
        compiler_params=_cparams(1),
        name="rw_prep",
    )(*args)
    return outs


def _rw_intra_kernel(rh_ref, ah_ref, bh_ref, kh_ref, bt_ref, kt_ref, v_ref, gc_ref,
                     wr_ref, mb_ref, uloc_ref, oloc_ref, kvloc_ref, gcol_ref, *, n_sub):
    n = RW_CHUNK
    ri = lax.broadcasted_iota(jnp.int32, (n, n), 0)
    ci = lax.broadcasted_iota(jnp.int32, (n, n), 1)
    strict = ci < ri
    incl = ci <= ri
    eye = ci == ri
    eye_b = eye.astype(BF16)
    ones_b = jnp.ones((n, n), BF16)
    items = [(c, h) for c in range(n_sub) for h in range(RW_HEADS)]
    rows = lambda c: slice(c * n, (c + 1) * n)
    lanes = lambda h: slice(h * RW_HEAD, (h + 1) * RW_HEAD)
    ld = lambda ref: [ref[rows(c), lanes(h)] for c, h in items]
    rh, ah, bh, kh, bt, kt, v = (ld(r) for r in (rh_ref, ah_ref, bh_ref, kh_ref, bt_ref, kt_ref, v_ref))
    ar = [jnp.concatenate([a, r], axis=0) for a, r in zip(ah, rh)]
    gb = [_mm_nt(x, y) for x, y in zip(ar, bh)]
    gk = [_mm_nt(x, y) for x, y in zip(ar, kh)]
    l_ab = [jnp.where(strict, g[:n], 0.0) for g in gb]
    m_rb =[jnp.where(incl, g[n:], 0.0).astype(BF16) for g in gb]
    lm_k = [jnp.concatenate([jnp.where(strict, g[:n], 0.0), jnp.where(incl, g[n:], 0.0)], axis=0).astype(BF16) for g in gk]
    akv = [_mm(x, y) for x, y in zip(lm_k, v)]
    tinv = [jnp.where(eye, 1.0, x) for x in l_ab]
    l_ab = [x.astype(BF16) for x in l_ab]
    pw = [_mm(x, x) for x in l_ab]
    for _ in range(4):
        pw_b = [x.astype(BF16) for x in pw]
        res = [_mm(jnp.concatenate([p, t.astype(BF16)], axis=0), p) for p, t in zip(pw_b, tinv)]
        pw = [r[:n] for r in res]
        tinv = [t + r[n:] for t, r in zip(tinv, res)]
    tinv = [(t + _mm(t, p)).astype(BF16) for t, p in zip(tinv, pw)]
    w = [_mm(t, a) for t, a in zip(tinv, ah)]
    uloc = [_mm(t, x[:n]) for t, x in zip(tinv, akv)]
    kvloc = [_mm_tn(x, y) for x, y in zip(kt, v)]
    bt_t = [_mm_tn(x, eye_b) for x in bt]
    for i, (c, h) in enumerate(items):
        decay = jnp.where(eye, jnp.broadcast_to(gc_ref[(c + 1) * n - 1:(c + 1) * n, lanes(h)], (n, n)), 0.0)
        gcol_ref[rows(c), lanes(h)] = _mm_sel_right(decay, ones_b)
        wr_ref[2 * c * n:(2 * c + 1) * n, lanes(h)] = w[i].astype(BF16)
        wr_ref[(2 * c + 1) * n:(2 * c + 2) * n, lanes(h)] = rh[i]
        mb_ref[2 * c * n:(2 * c + 1) * n, lanes(h)] = m_rb[i]
        mb_ref[(2 * c + 1) * n:(2 * c + 2) * n, lanes(h)] = bt_t[i].astype(BF16)
        uloc_ref[rows(c), lanes(h)] = uloc[i]
        oloc_ref[rows(c), lanes(h)] = akv[i][n:]
        kvloc_ref[rows(c), lanes(h)] = kvloc[i]


def _rw_seq_kernel(wr_ref, mb_ref, uloc_ref, oloc_ref, kvloc_ref, gcol_ref, y_ref, h_sc, *, n_batch):
    @pl.when(pl.program_id(0) == 0)
    def _():
        h_sc[...] = jnp.zeros(h_sc.shape, F32)

    n = RW_CHUNK
    items = [(b, h) for b in range(n_batch) for h in range(RW_HEADS)]
    lanes = lambda h: slice(h * RW_HEAD, (h + 1) * RW_HEAD)
    h0 = [h_sc[i] for i in range(len(items))]
    s1 = [_mm(wr_ref[b, :, lanes(h)], x) for (b, h), x in zip(items, h0)]
    u = [x[:n] + uloc_ref[b, :, lanes(h)] for (b, h), x in zip(items, s1)]
    s2 = [_mm(mb_ref[b, :, lanes(h)], x) for (b, h), x in zip(items, u)]
    for i, (b, h) in enumerate(items):
        y_ref[b, :, lanes(h)] = s1[i][n:] + s2[i][:n] + oloc_ref[b, :, lanes(h)]
        h_sc[i] = gcol_ref[b, :, lanes(h)] * h0[i] + s2[i][n:] + kvloc_ref[b, :, lanes(h)]


def _rw_scan(ops, n_batch, seq):
    t = n_batch * seq
    n_sub = 4
    rows = n_sub * RW_CHUNK
    spec = pl.BlockSpec((rows, RW_WIDTH), lambda i: (i, 0))
    spec2 = pl.BlockSpec((2 * rows, RW_WIDTH), lambda i: (i, 0))
    wr, mb, uloc, oloc, kvloc, gcol = pl.pallas_call(
        functools.partial(_rw_intra_kernel, n_sub=n_sub),
        grid=(t // rows,),
        in_specs=[spec] * 8,
        out_specs=[spec2, spec2, spec, spec, spec, spec],
        out_shape=[jax.ShapeDtypeStruct((2 * t, RW_WIDTH), BF16)] * 2 + [jax.ShapeDtypeStruct((t, RW_WIDTH), F32)] * 4,
        compiler_params=_cparams(1),
        name="rw_intra",
    )(*ops)
    n_chunk = seq // RW_CHUNK
    b3 = lambda a: a.reshape(n_batch, -1, RW_WIDTH)
    cspec = pl.BlockSpec((n_batch, RW_CHUNK, RW_WIDTH), lambda c: (0, c, 0))
    cspec2 = pl.BlockSpec((n_batch, 2 * RW_CHUNK, RW_WIDTH), lambda c: (0, c, 0))
    y = pl.pallas_call(
        functools.partial(_rw_seq_kernel, n_batch=n_batch),
        grid=(n_chunk,),
        in_specs=[cspec2, cspec2, cspec, cspec, cspec, cspec],
        out_specs=cspec,
        out_shape=jax.ShapeDtypeStruct((n_batch, seq, RW_WIDTH), F32),
        scratch_shapes=[pltpu.VMEM((n_batch * RW_HEADS, RW_HEAD, RW_HEAD), F32)],
        compiler_params=_cparams(1),
        name="rw_seq",
    )(b3(wr), b3(mb), b3(uloc), b3(oloc), b3(kvloc), b3(gcol))
    return y.reshape(t, RW_WIDTH)


def _merge_kernel(x_ref, ys5_ref, us5_ref, yb_ref, yrw_ref, bonus_ref, g_ref, gl_ref, d_ref, wglu_ref, lng_ref, lnb_ref,
                  e64_ref, wb_ref, wout_ref, o_ref):
    ya = ys5_ref[...] + d_ref[...] * us5_ref[...]
    ya = 0.5 * ya * (1.0 + jnp.tanh(math.sqrt(2.0 / math.pi) * (ya + 0.044715 * (ya * ya * ya))))
    ya = ya * _sigmoid(_mm(ya, wglu_ref[...]))
    y = yrw_ref[...]
    mean = _mm_sel_right(y, e64_ref[...]) * (1.0 / RW_HEAD)
    yc = y - mean
    var = _mm_sel_right(yc * yc, e64_ref[...]) * (1.0 / RW_HEAD)
    yc = yc * lax.rsqrt(var + RW_LN_EPS) * lng_ref[...] + lnb_ref[...]
    yc = (yc + bonus_ref[...]) * g_ref[...]
    merged = jnp.zeros((x_ref.shape[0], D_MODEL), F32)
    for n, br in enumerate((ya, yb_ref[...], yc)):
        gate = _sigmoid(gl_ref[:, n * D_MODEL:(n + 1) * D_MODEL])
        merged = merged + gate * _mm(br, wb_ref[n])
    o_ref[...] = x_ref[...] + _mm(merged, wout_ref[...])


def _merge(x2, ys5, us5, yb, yrw, bonus, g, gl, s5_d, w_glu, ln_g, ln_b, w_branch, w_out, tm):
    t = x2.shape[0]
    consts = [s5_d[None, :], w_glu.astype(BF16), ln_g[None, :], ln_b[None, :], _group_ones(RW_WIDTH, RW_HEAD),
              w_branch.astype(BF16), w_out.astype(BF16)]
    row = lambda width: pl.BlockSpec((tm, width), lambda i: (i, 0))
    widths = [D_MODEL, S5_WIDTH, S5_WIDTH, DA_WIDTH, RW_WIDTH, RW_WIDTH, RW_WIDTH, N_BRANCH * D_MODEL]
    return pl.pallas_call(
        _merge_kernel,
        grid=(t // tm,),
        in_specs=[row(wd) for wd in widths] + [_const_spec(c.shape) for c in consts],
        out_specs=row(D_MODEL),
        out_shape=jax.ShapeDtypeStruct((t, D_MODEL), F32),
        compiler_params=_cparams(1),
        name="merge",
    )(x2, ys5, us5, yb, yrw, bonus, g, gl, *consts)


def _ffn_kernel(x_ref, g_ref, win_ref, wout_ref, o_ref):
    x = x_ref[...]
    h = (x * lax.rsqrt(jnp.mean(x * x, axis=-1, keepdims=True) + NORM_EPS) * g_ref[...]).astype(BF16)
    gate = jnp.dot(h, win_ref[:, :D_FF], preferred_element_type=F32)
    up = jnp.dot(h, win_ref[:, D_FF:], preferred_element_type=F32)
    act = gate * _sigmoid(gate) * up
    o_ref[...] = x + _mm(act, wout_ref[...])


def _ffn(x2, g, w_in, w_out, tm):
    t = x2.shape[0]
    consts = [g[None, :], w_in.astype(BF16), w_out.astype(BF16)]
    row = pl.BlockSpec((tm, D_MODEL), lambda i: (i, 0))
    return pl.pallas_call(
        _ffn_kernel,
        grid=(t // tm,),
        in_specs=[row] + [_const_spec(c.shape) for c in consts],
        out_specs=row,
        out_shape=jax.ShapeDtypeStruct((t, D_MODEL), F32),
        compiler_params=_cparams(1),
        name="ffn",
    )(x2, *consts)


def _layer(x2, i, n_batch, seq, p, v_first, tm):
    us5, q, k, v, zrw, gl = _in_proj(x2, p['norm1_g'], p['w_in'], p['da_q_gain'], p['da_k_gain'], tm)
    n_double = max(1, (seq // S5_CHUNK - 1).bit_length())
    tables = _s5_tables(p['s5_lambda_re'], p['s5_lambda_im'], p['s5_log_dt'], p['s5_b_re'], p['s5_b_im'],
                        p['s5_c_re'], p['s5_c_im'], n_double)
    ys5 = _s5_scan(us5, tables, n_batch, seq)
    lambda_init = 0.8 - 0.6 * math.exp(-0.3 * i)
    yb = _diff_attn(q, k, v, p['da_lambda'], p['da_subln_g'], lambda_init, n_batch, seq)
    rw = {name[3:]: val for name, val in p.items() if name.startswith('rw_')}
    outs = _rw_prep(zrw, rw, v_first, seq, tm)
    if v_first is None:
        v_first = outs[10]
    yrw = _rw_scan(outs[:8], n_batch, seq)
    x2 = _merge(x2, ys5, us5, yb, yrw, outs[8], outs[9], gl, p['s5_d'], p['s5_w_glu'], p['rw_ln_g'], p['rw_ln_b'],
                p['w_branch'], p['w_out'], tm)
    x2 = _ffn(x2, p['norm2_g'], p['w_ffn_in'], p['w_ffn_out'], tm)
    return x2, v_first


def kernel(x, norm1_g, w_in, s5_lambda_re, s5_lambda_im, s5_log_dt, s5_b_re, s5_b_im, s5_c_re, s5_c_im, s5_d, s5_w_glu, da_q_gain, da_k_gain, da_lambda, da_subln_g, rw_mu, rw_w0, rw_w2, rw_a0, rw_a2, rw_g2, rw_k_k, rw_k_a, rw_r_k, rw_ln_g, rw_ln_b, rw_v0, rw_v1, rw_v2, w_branch, w_out, norm2_g, w_ffn_in, w_ffn_out):
    n_batch, seq, _ = x.shape
    per_layer = dict(norm1_g=norm1_g, w_in=w_in, s5_lambda_re=s5_lambda_re, s5_lambda_im=s5_lambda_im,
                     s5_log_dt=s5_log_dt, s5_b_re=s5_b_re, s5_b_im=s5_b_im, s5_c_re=s5_c_re, s5_c_im=s5_c_im,
                     s5_d=s5_d, s5_w_glu=s5_w_glu, da_q_gain=da_q_gain, da_k_gain=da_k_gain, da_lambda=da_lambda,
                     da_subln_g=da_subln_g, rw_mu=rw_mu, rw_w0=rw_w0, rw_w2=rw_w2, rw_a0=rw_a0, rw_a2=rw_a2,
                     rw_g2=rw_g2, rw_k_k=rw_k_k, rw_k_a=rw_k_a, rw_r_k=rw_r_k, rw_ln_g=rw_ln_g, rw_ln_b=rw_ln_b,
                     w_branch=w_branch, w_out=w_out, norm2_g=norm2_g, w_ffn_in=w_ffn_in, w_ffn_out=w_ffn_out)
    tm = min(256, seq)
    x2 = x.reshape(n_batch * seq, D_MODEL)
    v_first = None
    for i in range(w_in.shape[0]):
        p = {name: val[i] for name, val in per_layer.items()}
        if i > 0:
            p.update(rw_v0=rw_v0[i - 1], rw_v1=rw_v1[i - 1], rw_v2=rw_v2[i - 1])
        x2, v_first = _layer(x2, i, n_batch, seq, p, v_first, tm)
    return x2.reshape(x.shape)
```

```python
import functools
import math

import jax
import jax.numpy as jnp
from jax import lax
from jax.experimental import pallas as pl
from jax.experimental.pallas import tpu as pltpu

F32 = jnp.float32
BF16 = jnp.bfloat16

D_MODEL = 1024
DEPTH = 2
S5_WIDTH = 512
S5_GROUP = 16
S5_GROUPS = S5_WIDTH // S5_GROUP
S5_STATE = 64
S5_CHUNK = 16
S5_SLAB = 8
DA_HEADS = 4
DA_HEAD_DIM = 64
DA_V_DIM = 2 * DA_HEAD_DIM
DA_WIDTH = DA_HEADS * DA_V_DIM
DA_SUBLN_EPS = 1e-5
RW_HEAD = 64
RW_WIDTH = 512
RW_HEADS = RW_WIDTH // RW_HEAD
RW_DECAY_LORA = 32
RW_A_LORA = 32
RW_G_LORA = 96
RW_TAIL = RW_DECAY_LORA + RW_A_LORA + RW_G_LORA
RW_SHIFT_WIDTH = 3 * RW_WIDTH + RW_TAIL
RW_LN_EPS = 64e-5
RW_CHUNK = 64
N_BRANCH = 3
D_FF = 2816
NORM_EPS = 1e-6
NEG_BIG = -1e30
ATTN_TQ = 512
ATTN_STRIP = 256
TOKEN_TILE = 512
RW_PREP_TILE = 256

VMEM_LIMIT = 56 * 1024 * 1024


def _cparams(n_axes):
    return pltpu.CompilerParams(dimension_semantics=("arbitrary",) * n_axes, vmem_limit_bytes=VMEM_LIMIT)


def _const_spec(shape):
    nd = len(shape)
    return pl.BlockSpec(shape, lambda *_: (0,) * nd, pipeline_mode=pl.Buffered(1))


def _mm(a, b):
    return jnp.dot(a.astype(BF16), b.astype(BF16), preferred_element_type=F32)


def _mm_nt(a, b):
    return lax.dot_general(a.astype(BF16), b.astype(BF16), (((1,), (1,)), ((), ())), preferred_element_type=F32)


def _mm_tn(a, b):
    return lax.dot_general(a.astype(BF16), b.astype(BF16), (((0,), (0,)), ((), ())), preferred_element_type=F32)


def _split2(x):
    hi = x.astype(BF16)
    lo = (x - hi.astype(F32)).astype(BF16)
    return hi, lo


def _split3(x):
    hi = x.astype(BF16)
    r = x - hi.astype(F32)
    mid = r.astype(BF16)
    lo = (r - mid.astype(F32)).astype(BF16)
    return hi, mid, lo


def _mm_sel_right(x, sel, terms=2):
    d = lambda p: jnp.dot(p, sel, preferred_element_type=F32)
    return sum(d(p) for p in (_split2(x) if terms == 2 else _split3(x)))


def _mm_sel_left(sel, x, terms=2):
    d = lambda p: jnp.dot(sel, p, preferred_element_type=F32)
    return sum(d(p) for p in (_split2(x) if terms == 2 else _split3(x)))


def _mm_split(x, w_hi, w_lo):
    x_hi, x_lo = _split2(x)
    d = lambda p, q: jnp.dot(p, q, preferred_element_type=F32)
    return d(x_hi, w_hi) + d(x_hi, w_lo) + d(x_lo, w_hi)


def _sigmoid(x):
    return 1.0 / (1.0 + jnp.exp(-x))


def _group_ones(width, group):
    r = jnp.arange(width) // group
    return (r[:, None] == r[None, :]).astype(BF16)


def _in_proj_kernel(x_ref, g_ref, ws5_ref, wq_ref, wk_ref, wvt_ref, wrw_ref, wgate_ref, e64_ref, qg_ref, kg_ref,
                    us5_ref, q_ref, k_ref, vt_ref, zrw_ref, gl_ref):
    x = x_ref[...]
    h = x * lax.rsqrt(jnp.mean(x * x, axis=-1, keepdims=True) + NORM_EPS) * g_ref[...]
    h = h.astype(BF16)
    dot = lambda w_ref: jnp.dot(h, w_ref[...], preferred_element_type=F32)
    us5_ref[...] = dot(ws5_ref)

    def qk_norm(t, gain_ref):
        ms = _mm_sel_right(t * t, e64_ref[...]) * (1.0 / DA_HEAD_DIM)
        return t * lax.rsqrt(ms + NORM_EPS) * gain_ref[...]

    q_ref[...] = qk_norm(dot(wq_ref), qg_ref).astype(BF16)
    k_ref[...] = qk_norm(dot(wk_ref), kg_ref).astype(BF16)
    vt_ref[0] = lax.dot_general(wvt_ref[...], h, (((1,), (1,)), ((), ())), preferred_element_type=F32).astype(BF16)
    zrw_ref[...] = dot(wrw_ref)
    gl_ref[...] = dot(wgate_ref)


def _in_proj(x2, g, w, q_gain, k_gain, tm):
    t = x2.shape[0]
    w = w.astype(BF16)
    o = 0
    parts = []
    for width in (S5_WIDTH, DA_WIDTH, DA_WIDTH, DA_WIDTH, RW_SHIFT_WIDTH, N_BRANCH * D_MODEL):
        parts.append(w[:, o:o + width])
        o += width
    parts[3] = parts[3].T
    e64 = _group_ones(DA_WIDTH, DA_HEAD_DIM)
    qg = jnp.tile(q_gain.reshape(-1), DA_HEADS)[None, :] * (DA_HEAD_DIM ** -0.5 * math.log2(math.e))
    kg = jnp.tile(k_gain.reshape(-1), DA_HEADS)[None, :]
    row = lambda width: pl.BlockSpec((tm, width), lambda i: (i, 0))
    consts = [g[None, :]] + parts + [e64, qg, kg]
    out_specs = [row(S5_WIDTH), row(DA_WIDTH), row(DA_WIDTH), pl.BlockSpec((1, DA_WIDTH, tm), lambda i: (i, 0, 0)),
                 row(RW_SHIFT_WIDTH), row(N_BRANCH * D_MODEL)]
    out_shape = [jax.ShapeDtypeStruct((t, S5_WIDTH), F32), jax.ShapeDtypeStruct((t, DA_WIDTH), BF16),
                 jax.ShapeDtypeStruct((t, DA_WIDTH), BF16), jax.ShapeDtypeStruct((t // tm, DA_WIDTH, tm), BF16),
                 jax.ShapeDtypeStruct((t, RW_SHIFT_WIDTH), F32), jax.ShapeDtypeStruct((t, N_BRANCH * D_MODEL), F32)]
    return pl.pallas_call(
        _in_proj_kernel,
        grid=(t // tm,),
        in_specs=[row(D_MODEL)] + [_const_spec(c.shape) for c in consts],
        out_specs=out_specs,
        out_shape=out_shape,
        compiler_params=_cparams(1),
        name="in_proj",
    )(x2, *consts)


def _s5_tables(lam_re, lam_im, log_dt, b_re, b_im, c_re, c_im, n_double):
    n = S5_CHUNK
    dt = jnp.exp(log_dt)[:, None]
    den = lam_re * lam_re + lam_im * lam_im

    def a_pow(tau):
        mag = jnp.exp(lam_re * dt * tau)
        return mag * jnp.cos(lam_im * dt * tau), mag * jnp.sin(lam_im * dt * tau)

    ar, ai = a_pow(1.0)
    nr = ar - 1.0
    kr = (nr * lam_re + ai * lam_im) / den
    ki = (ai * lam_re - nr * lam_im) / den
    bbar_r = kr[..., None] * b_re - ki[..., None] * b_im
    bbar_i = kr[..., None] * b_im + ki[..., None] * b_re
    taus = jnp.arange(n + 1, dtype=lam_re.dtype)
    pr, pi = jax.vmap(a_pow)(taus)
    hp = lax.Precision.HIGHEST
    abr = pr[..., None] * bbar_r[None] - pi[..., None] * bbar_i[None]
    abi = pr[..., None] * bbar_i[None] + pi[..., None] * bbar_r[None]
    ktau = (jnp.einsum('gdp,ngpc->ngdc', c_re, abr, precision=hp)
            - jnp.einsum('gdp,ngpc->ngdc', c_im, abi, precision=hp))
    g, q, c, p2 = lam_re.shape[0], S5_SLAB, S5_GROUP, 2 * S5_STATE
    j = g // q
    eye = jnp.eye(q, dtype=ktau.dtype)
    kl = ktau[:n].transpose(1, 0, 3, 2).reshape(j, q, n, c, c).transpose(0, 2, 1, 3, 4)
    klag = (kl[:, :, :, :, None, :] * eye[None, None, :, None, :, None]).reshape(j, n, q * c, q * c)
    rev = n - 1 - jnp.arange(n)
    b_end = jnp.concatenate([abr[rev], abi[rev]], axis=2)
    bend = b_end.reshape(n, j, q, p2, c).transpose(1, 0, 2, 4, 3).reshape(j, n, q * c, p2)
    cr = c_re[None] * pr[1:, :, None, :] - c_im[None] * pi[1:, :, None, :]
    ci = c_re[None] * pi[1:, :, None, :] + c_im[None] * pr[1:, :, None, :]
    c_in = jnp.concatenate([cr, -ci], axis=3)
    cin = c_in.reshape(n, j, q, c, p2).transpose(1, 0, 4, 2, 3).reshape(j, n, p2, q * c)
    rows = []
    for k in range(n_double):
        sr, si = a_pow(float(n * 2 ** k))
        rows += [jnp.concatenate([sr, sr], axis=1), jnp.concatenate([-si, si], axis=1)]
    step = jnp.stack(rows, axis=1)
    step = step.reshape(j, q, 2 * n_double, p2).transpose(0, 2, 1, 3).reshape(j, 2 * n_double, q * p2)
    return klag.astype(BF16), bend.astype(BF16), cin.astype(BF16), step


def _s5_kernel(u_ref, klag_ref, bend_ref, cin_ref, st_ref, y_ref, tt_sc, bb_sc, cc_sc, *, n_double):
    n, w = S5_CHUNK, S5_SLAB * S5_GROUP

    @pl.when(pl.program_id(1) == 0)
    def _():
        zero = jnp.zeros((w, w), BF16)
        row_q = lax.broadcasted_iota(jnp.int32, (w, w), 0) // S5_GROUP
        lane_q = lax.broadcasted_iota(jnp.int32, (w, w), 1) // S5_GROUP
        for s in range(n):
            for t in range(n):
                tt_sc[s * w:(s + 1) * w, t * w:(t + 1) * w] = klag_ref[0, t - s] if t >= s else zero
            b_blk, c_blk = bend_ref[0, s], cin_ref[0, s]
            for q in range(S5_SLAB):
                bb_sc[s * w:(s + 1) * w, q * w:(q + 1) * w] = jnp.where(row_q == q, b_blk, zero)
                cc_sc[q * w:(q + 1) * w, s * w:(s + 1) * w] = jnp.where(lane_q == q, c_blk, zero)

    u = jnp.concatenate([u_ref[:, s, :].astype(BF16) for s in range(n)], axis=1)
    x = jnp.dot(u, bb_sc[...], preferred_element_type=F32)
    row = lax.broadcasted_iota(jnp.int32, x.shape, 0)
    is_re = lax.broadcasted_iota(jnp.int32, x.shape, 1) % (2 * S5_STATE) < S5_STATE
    width = x.shape[1]
    for k in range(n_double):
        sh = jnp.where(row >= 2 ** k, pltpu.roll(x, 2 ** k, 0), 0.0)
        swapped = jnp.where(is_re, pltpu.roll(sh, width - S5_STATE, 1), pltpu.roll(sh, S5_STATE, 1))
        x = x + st_ref[0, 2 * k:2 * k + 1, :] * sh + st_ref[0, 2 * k + 1:2 * k + 2, :] * swapped
    xin = jnp.where(row >= 1, pltpu.roll(x, 1, 0), 0.0).astype(BF16)
    y = jnp.dot(u, tt_sc[...], preferred_element_type=F32) + jnp.dot(xin, cc_sc[...], preferred_element_type=F32)
    for t in range(n):
        y_ref[:, t, :] = y[:, t * w:(t + 1) * w]


def _s5_scan(u, tables, n_batch, seq):
    klag, bend, cin, st = tables
    n_chunk = seq // S5_CHUNK
    n_slab = klag.shape[0]
    w = S5_SLAB * S5_GROUP
    u3 = u.reshape(n_batch * n_chunk, S5_CHUNK, S5_WIDTH)
    per_slab = lambda a: pl.BlockSpec((1,) + a.shape[1:], lambda j, b: (j,) + (0,) * (a.ndim - 1))
    io_spec = pl.BlockSpec((n_chunk, S5_CHUNK, w), lambda j, b: (b, 0, j))
    y = pl.pallas_call(
        functools.partial(_s5_kernel, n_double=st.shape[1] // 2),
        grid=(n_slab, n_batch),
        in_specs=[io_spec, per_slab(klag), per_slab(bend), per_slab(cin), per_slab(st)],
        out_specs=io_spec,
        out_shape=jax.ShapeDtypeStruct(u3.shape, F32),
        scratch_shapes=[pltpu.VMEM((S5_CHUNK * w, S5_CHUNK * w), BF16),
                        pltpu.VMEM((S5_CHUNK * w, S5_SLAB * 2 * S5_STATE), BF16),
                        pltpu.VMEM((S5_SLAB * 2 * S5_STATE, S5_CHUNK * w), BF16)],
        compiler_params=_cparams(2),
        name="s5_chunk",
    )(u3, klag, bend, cin, st)
    return y.reshape(u.shape)


def _attn_kernel(lam_ref, sg_ref, q_ref, k_ref, vt_ref, o_ref, m_sc, acc_sc, *, tq, lambda_init):
    tk = vt_ref.shape[2]
    qi = pl.program_id(2)
    q = q_ref[...]
    lane = lax.broadcasted_iota(jnp.int32, q.shape, 1)
    zero = jnp.zeros_like(q)
    q2 = jnp.concatenate([jnp.where(lane < DA_HEAD_DIM, q, zero), jnp.where(lane >= DA_HEAD_DIM, q, zero)], axis=0)
    m_sc[...] = jnp.full(m_sc.shape, NEG_BIG, F32)
    acc_sc[...] = jnp.zeros(acc_sc.shape, F32)

    n_strip = 2 * tq // ATTN_STRIP
    n_sub = tq // tk
    ones_rows = jnp.ones((16, tk), BF16)

    def step(j, masked):
        kb = k_ref[pl.ds(pl.multiple_of(j * tq, tq), tq), :]
        vtb = [jnp.concatenate([vt_ref[j * n_sub + i], ones_rows], axis=0) for i in range(n_sub)]
        s = [lax.dot_general(kb, q2[c * ATTN_STRIP:(c + 1) * ATTN_STRIP], (((1,), (1,)), ((), ())),
                             preferred_element_type=F32) for c in range(n_strip)]
        for c in range(n_strip):
            cs = slice(c * ATTN_STRIP, (c + 1) * ATTN_STRIP)
            sc = s[c]
            if masked:
                key = lax.broadcasted_iota(jnp.int32, sc.shape, 0)
                col = lax.broadcasted_iota(jnp.int32, sc.shape, 1)
                sc = jnp.where(key <= col + (c * ATTN_STRIP) % tq, sc, NEG_BIG)
            m_prev = m_sc[:, cs]
            m_next = jnp.maximum(m_prev, jnp.max(sc, axis=0, keepdims=True))
            alpha = jnp.exp2(m_prev - m_next)
            p = jnp.exp2(sc - m_next).astype(BF16)
            m_sc[:, cs] = m_next
            pv = sum(jnp.dot(vtb[i], p[i * tk:(i + 1) * tk], preferred_element_type=F32) for i in range(n_sub))
            acc_sc[:, cs] = alpha * acc_sc[:, cs] + pv

    def body(jj, carry):
        step(2 * jj, False)
        step(2 * jj + 1, False)
        return carry

    lax.fori_loop(0, qi // 2, body, 0)

    @pl.when(qi % 2 == 1)
    def _():
        step(qi - 1, False)

    step(qi, True)

    lv = lam_ref[...]
    lam = (jnp.exp(jnp.sum(lv[0:1] * lv[1:2], axis=1, keepdims=True))
           - jnp.exp(jnp.sum(lv[2:3] * lv[3:4], axis=1, keepdims=True)) + lambda_init)
    o_all = acc_sc[:DA_V_DIM, :] / acc_sc[DA_V_DIM:DA_V_DIM + 1, :]
    o = (o_all[:, :tq] - lam * o_all[:, tq:]).T
    o = o * lax.rsqrt(jnp.mean(o * o, axis=-1, keepdims=True) + DA_SUBLN_EPS) * sg_ref[...] * (1.0 - lambda_init)
    o_ref[...] = o.astype(o_ref.dtype)


def _diff_attn(q, k, vt, lam_vecs, subln_g, lambda_init, n_batch, seq):
    tk = vt.shape[2]
    tq = min(ATTN_TQ, seq)
    nq = seq // tq
    q_spec = pl.BlockSpec((tq, DA_V_DIM), lambda b, h, i: (b * nq + i, h))
    k_spec = pl.BlockSpec((seq, DA_V_DIM), lambda b, h, i: (b, h))
    vt_spec = pl.BlockSpec((seq // tk, DA_V_DIM, tk), lambda b, h, i: (b, h, 0))
    return pl.pallas_call(
        functools.partial(_attn_kernel, tq=tq, lambda_init=lambda_init),
        grid=(n_batch, DA_HEADS, nq),
        in_specs=[_const_spec(lam_vecs.shape), _const_spec((1, DA_V_DIM)), q_spec, k_spec, vt_spec],
        out_specs=q_spec,
        out_shape=jax.ShapeDtypeStruct((n_batch * seq, DA_WIDTH), BF16),
        scratch_shapes=[pltpu.VMEM((1, 2 * tq), F32), pltpu.VMEM((DA_V_DIM + 16, 2 * tq), F32)],
        compiler_params=_cparams(3),
        name="diff_attn",
    )(lam_vecs, subln_g[None, :], q, k, vt)


def _rw_prep_kernel(*refs, tm, seq, gated):
    if gated:
        (z_ref, zp_ref, vf_ref, mu_ref, w0_ref, a0_ref, kk_ref, ka_ref, rk_ref, w2h_ref, w2l_ref, a2h_ref, a2l_ref,
         g2h_ref, g2l_ref, e64_ref, tri_ref, v0_ref, v1h_ref, v1l_ref, v2h_ref, v2l_ref,
         rh_ref, ah_ref, bh_ref, kh_ref, bt_ref, kt_ref, v_ref, gc_ref, bonus_ref, g_ref) = refs
    else:
        (z_ref, zp_ref, mu_ref, w0_ref, a0_ref, kk_ref, ka_ref, rk_ref, w2h_ref, w2l_ref, a2h_ref, a2l_ref,
         g2h_ref, g2l_ref, e64_ref, tri_ref,
         rh_ref, ah_ref, bh_ref, kh_ref, bt_ref, kt_ref, v_ref, gc_ref, bonus_ref, g_ref, vfirst_ref) = refs
    i = pl.program_id(0)
    z = z_ref[...]
    first = jnp.where((i * tm) % seq == 0, 0.0, 1.0) * zp_ref[7:8, :]
    row = lax.broadcasted_iota(jnp.int32, z.shape, 0)
    prev = jnp.where(row == 0, first, pltpu.roll(z, 1, 0))
    zf = z + (prev - z) * mu_ref[...]
    r = zf[:, :RW_WIDTH]
    k = zf[:, RW_WIDTH:2 * RW_WIDTH]
    v = zf[:, 2 * RW_WIDTH:3 * RW_WIDTH]
    tail = zf[:, 3 * RW_WIDTH:]
    w_in = w0_ref[...] + _mm_split(jnp.tanh(tail), w2h_ref[...], w2l_ref[...])
    softplus = jnp.maximum(-w_in, 0.0) + jnp.log(1.0 + jnp.exp(-jnp.abs(w_in)))
    logw = -jnp.exp(-softplus - 0.5)
    a = _sigmoid(a0_ref[...] + _mm_split(tail, a2h_ref[...], a2l_ref[...]))
    g_ref[...] = _mm_split(_sigmoid(tail), g2h_ref[...], g2l_ref[...])
    kk = k * kk_ref[...]
    norm = jnp.sqrt(_mm_sel_right(kk * kk, e64_ref[...]))
    kk = kk / jnp.maximum(norm, 1e-12)
    k = k * (1.0 + (a - 1.0) * ka_ref[...])
    if gated:
        lo = _mm_split(v, v1h_ref[...], v1l_ref[...])
        mix = _sigmoid(v0_ref[...] + _mm_split(lo, v2h_ref[...], v2l_ref[...]))
        v = v + (vf_ref[...] - v) * mix
    else:
        vfirst_ref[...] = v
    bonus_ref[...] = _mm_sel_right(r * k * rk_ref[...], e64_ref[...]) * v
    cs = _mm_sel_left(tri_ref[...], logw)
    cs_end = jnp.concatenate(
        [jnp.broadcast_to(cs[c * RW_CHUNK + RW_CHUNK - 1:(c + 1) * RW_CHUNK, :], (RW_CHUNK, RW_WIDTH))
         for c in range(tm // RW_CHUNK)], axis=0)
    inv = jnp.exp(-cs)
    to_end = jnp.exp(cs_end - cs)
    b = kk * a
    rh_ref[...] = (r * jnp.exp(cs)).astype(BF16)
    ah_ref[...] = (-kk * jnp.exp(cs - logw)).astype(BF16)
    bh_ref[...] = (b * inv).astype(BF16)
    kh_ref[...] = (k * inv).astype(BF16)
    bt_ref[...] = (b * to_end).astype(BF16)
    kt_ref[...] = (k * to_end).astype(BF16)
    v_ref[...] = v.astype(BF16)
    gc_ref[...] = jnp.exp(cs_end)


def _pad_rows(w, start):
    return jnp.zeros((RW_TAIL, w.shape[1]), w.dtype).at[start:start + w.shape[0]].set(w)


def _hi_lo(w):
    hi = w.astype(BF16)
    return hi, (w - hi.astype(w.dtype)).astype(BF16)


def _rw_prep(z, p, v_first, seq, tm):
    t = z.shape[0]
    gated = v_first is not None
    idx = jnp.arange(tm)
    same_chunk = (idx[:, None] // RW_CHUNK) == (idx[None, :] // RW_CHUNK)
    tri = (same_chunk & (idx[None, :] <= idx[:, None])).astype(BF16)
    vec = lambda a: a[None, :]
    consts = [vec(p['mu']), vec(p['w0']), vec(p['a0']), vec(p['k_k']), vec(p['k_a']), vec(p['r_k'].reshape(-1)),
              *_hi_lo(_pad_rows(p['w2'], 0)), *_hi_lo(_pad_rows(p['a2'], RW_DECAY_LORA)),
              *_hi_lo(_pad_rows(p['g2'], RW_DECAY_LORA + RW_A_LORA)),
              _group_ones(RW_WIDTH, RW_HEAD), tri]
    if gated:
        consts += [vec(p['v0']), *_hi_lo(p['v1']), *_hi_lo(p['v2'])]
    row = lambda width: pl.BlockSpec((tm, width), lambda i: (i, 0))
    zp_spec = pl.BlockSpec((8, RW_SHIFT_WIDTH), lambda i: (jnp.maximum(i * (tm // 8) - 1, 0), 0))
    in_specs = [row(RW_SHIFT_WIDTH), zp_spec] + ([row(RW_WIDTH)] if gated else []) + [_const_spec(c.shape) for c in consts]
    args = [z, z] + ([v_first] if gated else []) + consts
    n_out = 10 if gated else 11
    dtypes = [BF16] * 7 + [F32] * 3 + ([] if gated else [F32])
    outs = pl.pallas_call(
        functools.partial(_rw_prep_kernel, tm=tm, seq=seq, gated=gated),
        grid=(t // tm,),
        in_specs=in_specs,
        out_specs=[row(RW_WIDTH)] * n_out,
        out_shape=[jax.ShapeDtypeStruct((t, RW_WIDTH), dt) for dt in dtypes],
        compiler_params=_cparams(1),
        name="rw_prep",
    )(*args)
    return outs


def _rw_intra_kernel(rh_ref, ah_ref, bh_ref, kh_ref, bt_ref, kt_ref, v_ref, gc_ref,
                     wr_ref, mb_ref, uloc_ref, oloc_ref, kvloc_ref, gcol_ref, *, n_sub):
    n = RW_CHUNK
    ri = lax.broadcasted_iota(jnp.int32, (n, n), 0)
    ci = lax.broadcasted_iota(jnp.int32, (n, n), 1)
    strict = ci < ri
    incl = ci <= ri
    eye = ci == ri
    eye_b = eye.astype(BF16)
    ones_b = jnp.ones((n, n), BF16)
    items = [(c, h) for c in range(n_sub) for h in range(RW_HEADS)]
    rows = lambda c: slice(c * n, (c + 1) * n)
    lanes = lambda h: slice(h * RW_HEAD, (h + 1) * RW_HEAD)
    ld = lambda ref: [ref[rows(c), lanes(h)] for c, h in items]
    rh, ah, bh, kh, bt, kt, v = (ld(r) for r in (rh_ref, ah_ref, bh_ref, kh_ref, bt_ref, kt_ref, v_ref))
    ar = [jnp.concatenate([a, r], axis=0) for a, r in zip(ah, rh)]
    gb = [_mm_nt(x, y) for x, y in zip(ar, bh)]
    gk = [_mm_nt(x, y) for x, y in zip(ar, kh)]
    l_ab = [jnp.where(strict, g[:n], 0.0) for g in gb]
    m_rb =[jnp.where(incl, g[n:], 0.0).astype(BF16) for g in gb]
    lm_k = [jnp.concatenate([jnp.where(strict, g[:n], 0.0), jnp.where(incl, g[n:], 0.0)], axis=0).astype(BF16) for g in gk]
    akv = [_mm(x, y) for x, y in zip(lm_k, v)]
    tinv = [jnp.where(eye, 1.0, x) for x in l_ab]
    l_ab = [x.astype(BF16) for x in l_ab]
    pw = [_mm(x, x) for x in l_ab]
    for _ in range(4):
        pw_b = [x.astype(BF16) for x in pw]
        res = [_mm(jnp.concatenate([p, t.astype(BF16)], axis=0), p) for p, t in zip(pw_b, tinv)]
        pw = [r[:n] for r in res]
        tinv = [t + r[n:] for t, r in zip(tinv, res)]
    tinv = [(t + _mm(t, p)).astype(BF16) for t, p in zip(tinv, pw)]
    w = [_mm(t, a) for t, a in zip(tinv, ah)]
    uloc = [_mm(t, x[:n]) for t, x in zip(tinv, akv)]
    kvloc = [_mm_tn(x, y) for x, y in zip(kt, v)]
    bt_t = [_mm_tn(x, eye_b) for x in bt]
    for i, (c, h) in enumerate(items):
        decay = jnp.where(eye, jnp.broadcast_to(gc_ref[(c + 1) * n - 1:(c + 1) * n, lanes(h)], (n, n)), 0.0)
        gcol_ref[rows(c), lanes(h)] = _mm_sel_right(decay, ones_b, terms=3)
        wr_ref[2 * c * n:(2 * c + 1) * n, lanes(h)] = w[i].astype(BF16)
        wr_ref[(2 * c + 1) * n:(2 * c + 2) * n, lanes(h)] = rh[i]
        mb_ref[2 * c * n:(2 * c + 1) * n, lanes(h)] = m_rb[i]
        mb_ref[(2 * c + 1) * n:(2 * c + 2) * n, lanes(h)] = bt_t[i].astype(BF16)
        uloc_ref[rows(c), lanes(h)] = uloc[i]
        oloc_ref[rows(c), lanes(h)] = akv[i][n:]
        kvloc_ref[rows(c), lanes(h)] = kvloc[i]


def _rw_seq_kernel(wr_ref, mb_ref, uloc_ref, oloc_ref, kvloc_ref, gcol_ref, y_ref, h_sc, *, n_batch):
    @pl.when(pl.program_id(0) == 0)
    def _():
        h_sc[...] = jnp.zeros(h_sc.shape, F32)

    n = RW_CHUNK
    items = [(b, h) for b in range(n_batch) for h in range(RW_HEADS)]
    lanes = lambda h: slice(h * RW_HEAD, (h + 1) * RW_HEAD)
    h0 = [h_sc[i] for i in range(len(items))]
    s1 = [_mm(wr_ref[b, :, lanes(h)], x) for (b, h), x in zip(items, h0)]
    u = [x[:n] + uloc_ref[b, :, lanes(h)] for (b, h), x in zip(items, s1)]
    s2 = [_mm(mb_ref[b, :, lanes(h)], x) for (b, h), x in zip(items, u)]
    for i, (b, h) in enumerate(items):
        y_ref[b, :, lanes(h)] = s1[i][n:] + s2[i][:n] + oloc_ref[b, :, lanes(h)]
        h_sc[i] = gcol_ref[b, :, lanes(h)] * h0[i] + s2[i][n:] + kvloc_ref[b, :, lanes(h)]


def _rw_scan(ops, n_batch, seq):
    t = n_batch * seq
    n_sub = 4
    rows = n_sub * RW_CHUNK
    spec = pl.BlockSpec((rows, RW_WIDTH), lambda i: (i, 0))
    spec2 = pl.BlockSpec((2 * rows, RW_WIDTH), lambda i: (i, 0))
    wr, mb, uloc, oloc, kvloc, gcol = pl.pallas_call(
        functools.partial(_rw_intra_kernel, n_sub=n_sub),
        grid=(t // rows,),
        in_specs=[spec] * 8,
        out_specs=[spec2, spec2, spec, spec, spec, spec],
        out_shape=[jax.ShapeDtypeStruct((2 * t, RW_WIDTH), BF16)] * 2 + [jax.ShapeDtypeStruct((t, RW_WIDTH), F32)] * 4,
        compiler_params=_cparams(1),
        name="rw_intra",
    )(*ops)
    n_chunk = seq // RW_CHUNK
    b3 = lambda a: a.reshape(n_batch, -1, RW_WIDTH)
    cspec = pl.BlockSpec((n_batch, RW_CHUNK, RW_WIDTH), lambda c: (0, c, 0))
    cspec2 = pl.BlockSpec((n_batch, 2 * RW_CHUNK, RW_WIDTH), lambda c: (0, c, 0))
    y = pl.pallas_call(
        functools.partial(_rw_seq_kernel, n_batch=n_batch),
        grid=(n_chunk,),
        in_specs=[cspec2, cspec2, cspec, cspec, cspec, cspec],
        out_specs=cspec,
        out_shape=jax.ShapeDtypeStruct((n_batch, seq, RW_WIDTH), F32),
        scratch_shapes=[pltpu.VMEM((n_batch * RW_HEADS, RW_HEAD, RW_HEAD), F32)],
        compiler_params=_cparams(1),
        name="rw_seq",
    )(b3(wr), b3(mb), b3(uloc), b3(oloc), b3(kvloc), b3(gcol))
    return y.reshape(t, RW_WIDTH)


def _merge_kernel(x_ref, ys5_ref, us5_ref, yb_ref, yrw_ref, bonus_ref, g_ref, gl_ref, d_ref, wglu_ref, lng_ref, lnb_ref,
                  e64_ref, wb_ref, wout_ref, o_ref):
    ya = ys5_ref[...] + d_ref[...] * us5_ref[...]
    ya = 0.5 * ya * (1.0 + jnp.tanh(math.sqrt(2.0 / math.pi) * (ya + 0.044715 * (ya * ya * ya))))
    ya = ya * _sigmoid(_mm(ya, wglu_ref[...]))
    y = yrw_ref[...]
    mean = _mm_sel_right(y, e64_ref[...]) * (1.0 / RW_HEAD)
    yc = y - mean
    var = _mm_sel_right(yc * yc, e64_ref[...]) * (1.0 / RW_HEAD)
    yc = yc * lax.rsqrt(var + RW_LN_EPS) * lng_ref[...] + lnb_ref[...]
    yc = (yc + bonus_ref[...]) * g_ref[...]
    merged = jnp.zeros((x_ref.shape[0], D_MODEL), F32)
    for n, br in enumerate((ya, yb_ref[...], yc)):
        gate = _sigmoid(gl_ref[:, n * D_MODEL:(n + 1) * D_MODEL])
        merged = merged + gate * _mm(br, wb_ref[n])
    o_ref[...] = x_ref[...] + _mm(merged, wout_ref[...])


def _merge(x2, ys5, us5, yb, yrw, bonus, g, gl, s5_d, w_glu, ln_g, ln_b, w_branch, w_out, tm):
    t = x2.shape[0]
    consts = [s5_d[None, :], w_glu.astype(BF16), ln_g[None, :], ln_b[None, :], _group_ones(RW_WIDTH, RW_HEAD),
              w_branch.astype(BF16), w_out.astype(BF16)]
    row = lambda width: pl.BlockSpec((tm, width), lambda i: (i, 0))
    widths = [D_MODEL, S5_WIDTH, S5_WIDTH, DA_WIDTH, RW_WIDTH, RW_WIDTH, RW_WIDTH, N_BRANCH * D_MODEL]
    return pl.pallas_call(
        _merge_kernel,
        grid=(t // tm,),
        in_specs=[row(wd) for wd in widths] + [_const_spec(c.shape) for c in consts],
        out_specs=row(D_MODEL),
        out_shape=jax.ShapeDtypeStruct((t, D_MODEL), F32),
        compiler_params=_cparams(1),
        name="merge",
    )(x2, ys5, us5, yb, yrw, bonus, g, gl, *consts)


def _ffn_kernel(x_ref, g_ref, win_ref, wout_ref, o_ref):
    x = x_ref[...]
    h = (x * lax.rsqrt(jnp.mean(x * x, axis=-1, keepdims=True) + NORM_EPS) * g_ref[...]).astype(BF16)
    gate = jnp.dot(h, win_ref[:, :D_FF], preferred_element_type=F32)
    up = jnp.dot(h, win_ref[:, D_FF:], preferred_element_type=F32)
    act = gate * _sigmoid(gate) * up
    o_ref[...] = x + _mm(act, wout_ref[...])


def _ffn(x2, g, w_in, w_out, tm):
    t = x2.shape[0]
    consts = [g[None, :], w_in.astype(BF16), w_out.astype(BF16)]
    row = pl.BlockSpec((tm, D_MODEL), lambda i: (i, 0))
    return pl.pallas_call(
        _ffn_kernel,
        grid=(t // tm,),
        in_specs=[row] + [_const_spec(c.shape) for c in consts],
        out_specs=row,
        out_shape=jax.ShapeDtypeStruct((t, D_MODEL), F32),
        compiler_params=_cparams(1),
        name="ffn",
    )(x2, *consts)


def _layer(x2, i, n_batch, seq, p, v_first, tm):
    us5, q, k, v, zrw, gl = _in_proj(x2, p['norm1_g'], p['w_in'], p['da_q_gain'], p['da_k_gain'], tm)
    n_double = max(1, (seq // S5_CHUNK - 1).bit_length())
    tables = _s5_tables(p['s5_lambda_re'], p['s5_lambda_im'], p['s5_log_dt'], p['s5_b_re'], p['s5_b_im'],
                        p['s5_c_re'], p['s5_c_im'], n_double)
    ys5 = _s5_scan(us5, tables, n_batch, seq)
    lambda_init = 0.8 - 0.6 * math.exp(-0.3 * i)
    yb = _diff_attn(q, k, v, p['da_lambda'], p['da_subln_g'], lambda_init, n_batch, seq)
    rw = {name[3:]: val for name, val in p.items() if name.startswith('rw_')}
    outs = _rw_prep(zrw, rw, v_first, seq, min(RW_PREP_TILE, seq))
    if v_first is None:
        v_first = outs[10]
    yrw = _rw_scan(outs[:8], n_batch, seq)
    x2 = _merge(x2, ys5, us5, yb, yrw, outs[8], outs[9], gl, p['s5_d'], p['s5_w_glu'], p['rw_ln_g'], p['rw_ln_b'],
                p['w_branch'], p['w_out'], tm)
    x2 = _ffn(x2, p['norm2_g'], p['w_ffn_in'], p['w_ffn_out'], tm)
    return x2, v_first


def kernel(x, norm1_g, w_in, s5_lambda_re, s5_lambda_im, s5_log_dt, s5_b_re, s5_b_im, s5_c_re, s5_c_im, s5_d, s5_w_glu, da_q_gain, da_k_gain, da_lambda, da_subln_g, rw_mu, rw_w0, rw_w2, rw_a0, rw_a2, rw_g2, rw_k_k, rw_k_a, rw_r_k, rw_ln_g, rw_ln_b, rw_v0, rw_v1, rw_v2, w_branch, w_out, norm2_g, w_ffn_in, w_ffn_out):
    n_batch, seq, _ = x.shape
    per_layer = dict(norm1_g=norm1_g, w_in=w_in, s5_lambda_re=s5_lambda_re, s5_lambda_im=s5_lambda_im,
                     s5_log_dt=s5_log_dt, s5_b_re=s5_b_re, s5_b_im=s5_b_im, s5_c_re=s5_c_re, s5_c_im=s5_c_im,
                     s5_d=s5_d, s5_w_glu=s5_w_glu, da_q_gain=da_q_gain, da_k_gain=da_k_gain, da_lambda=da_lambda,
                     da_subln_g=da_subln_g, rw_mu=rw_mu, rw_w0=rw_w0, rw_w2=rw_w2, rw_a0=rw_a0, rw_a2=rw_a2,
                     rw_g2=rw_g2, rw_k_k=rw_k_k, rw_k_a=rw_k_a, rw_r_k=rw_r_k, rw_ln_g=rw_ln_g, rw_ln_b=rw_ln_b,
                     w_branch=w_branch, w_out=w_out, norm2_g=norm2_g, w_ffn_in=w_ffn_in, w_ffn_out=w_ffn_out)
    tm = min(TOKEN_TILE, seq)
    x2 = x.reshape(n_batch * seq, D_MODEL)
    v_first = None
    for i in range(w_in.shape[0]):
        p = {name: val[i] for name, val in per_layer.items()}
        if i > 0:
            p.update(rw_v0=rw_v0[i - 1], rw_v1=rw_v1[i - 1], rw_v2=rw_v2[i - 1])
        x2, v_first = _layer(x2, i, n_batch, seq, p, v_first, tm)
    return x2.reshape(x.shape)
```

```python
import functools
import math

import jax
import jax.numpy as jnp
from jax import lax
from jax.experimental import pallas as pl
from jax.experimental.pallas import tpu as pltpu

F32 = jnp.float32
BF16 = jnp.bfloat16

D_MODEL = 1024
DEPTH = 2
S5_WIDTH = 512
S5_GROUP = 16
S5_GROUPS = S5_WIDTH // S5_GROUP
S5_STATE = 64
S5_CHUNK = 16
S5_SLAB = 8
DA_HEADS = 4
DA_HEAD_DIM = 64
DA_V_DIM = 2 * DA_HEAD_DIM
DA_WIDTH = DA_HEADS * DA_V_DIM
DA_SUBLN_EPS = 1e-5
RW_HEAD = 64
RW_WIDTH = 512
RW_HEADS = RW_WIDTH // RW_HEAD
RW_DECAY_LORA = 32
RW_A_LORA = 32
RW_G_LORA = 96
RW_TAIL = RW_DECAY_LORA + RW_A_LORA + RW_G_LORA
RW_SHIFT_WIDTH = 3 * RW_WIDTH + RW_TAIL
RW_LN_EPS = 64e-5
RW_CHUNK = 64
N_BRANCH = 3
D_FF = 2816
NORM_EPS = 1e-6
NEG_BIG = -1e30
ATTN_TQ = 512
ATTN_STRIP = 256
TOKEN_TILE = 512
RW_PREP_TILE = 256

VMEM_LIMIT = 56 * 1024 * 1024


def _cparams(n_axes):
    return pltpu.CompilerParams(dimension_semantics=("arbitrary",) * n_axes, vmem_limit_bytes=VMEM_LIMIT)


def _const_spec(shape):
    nd = len(shape)
    return pl.BlockSpec(shape, lambda *_: (0,) * nd, pipeline_mode=pl.Buffered(1))


def _mm(a, b):
    return jnp.dot(a.astype(BF16), b.astype(BF16), preferred_element_type=F32)


def _mm_nt(a, b):
    return lax.dot_general(a.astype(BF16), b.astype(BF16), (((1,), (1,)), ((), ())), preferred_element_type=F32)


def _mm_tn(a, b):
    return lax.dot_general(a.astype(BF16), b.astype(BF16), (((0,), (0,)), ((), ())), preferred_element_type=F32)


def _split2(x):
    hi = x.astype(BF16)
    lo = (x - hi.astype(F32)).astype(BF16)
    return hi, lo


def _split3(x):
    hi = x.astype(BF16)
    r = x - hi.astype(F32)
    mid = r.astype(BF16)
    lo = (r - mid.astype(F32)).astype(BF16)
    return hi, mid, lo


def _mm_sel_right(x, sel, terms=2):
    d = lambda p: jnp.dot(p, sel, preferred_element_type=F32)
    return sum(d(p) for p in (_split2(x) if terms == 2 else _split3(x)))


def _mm_sel_left(sel, x, terms=2):
    d = lambda p: jnp.dot(sel, p, preferred_element_type=F32)
    return sum(d(p) for p in (_split2(x) if terms == 2 else _split3(x)))


def _mm_split(x, w_hi, w_lo):
    x_hi, x_lo = _split2(x)
    d = lambda p, q: jnp.dot(p, q, preferred_element_type=F32)
    return d(x_hi, w_hi) + d(x_hi, w_lo) + d(x_lo, w_hi)


def _sigmoid(x):
    return 1.0 / (1.0 + jnp.exp(-x))


def _group_ones(width, group):
    r = jnp.arange(width) // group
    return (r[:, None] == r[None, :]).astype(BF16)


def _in_proj_kernel(x_ref, g_ref, ws5_ref, wq_ref, wk_ref, wvt_ref, wrw_ref, wgate_ref, e64_ref, qg_ref, kg_ref,
                    us5_ref, q_ref, k_ref, vt_ref, zrw_ref, gl_ref):
    x = x_ref[...]
    h = x * lax.rsqrt(jnp.mean(x * x, axis=-1, keepdims=True) + NORM_EPS) * g_ref[...]
    h = h.astype(BF16)
    dot = lambda w_ref: jnp.dot(h, w_ref[...], preferred_element_type=F32)
    us5_ref[...] = dot(ws5_ref)

    def qk_norm(t, gain_ref):
        ms = _mm_sel_right(t * t, e64_ref[...]) * (1.0 / DA_HEAD_DIM)
        return t * lax.rsqrt(ms + NORM_EPS) * gain_ref[...]

    q_ref[...] = qk_norm(dot(wq_ref), qg_ref).astype(BF16)
    k_ref[...] = qk_norm(dot(wk_ref), kg_ref).astype(BF16)
    vt_ref[0] = lax.dot_general(wvt_ref[...], h, (((1,), (1,)), ((), ())), preferred_element_type=F32).astype(BF16)
    zrw_ref[...] = dot(wrw_ref)
    gl_ref[...] = dot(wgate_ref)


def _in_proj(x2, g, w, q_gain, k_gain, tm):
    t = x2.shape[0]
    w = w.astype(BF16)
    o = 0
    parts = []
    for width in (S5_WIDTH, DA_WIDTH, DA_WIDTH, DA_WIDTH, RW_SHIFT_WIDTH, N_BRANCH * D_MODEL):
        parts.append(w[:, o:o + width])
        o += width
    parts[3] = parts[3].T
    e64 = _group_ones(DA_WIDTH, DA_HEAD_DIM)
    qg = jnp.tile(q_gain.reshape(-1), DA_HEADS)[None, :] * (DA_HEAD_DIM ** -0.5 * math.log2(math.e))
    kg = jnp.tile(k_gain.reshape(-1), DA_HEADS)[None, :]
    row = lambda width: pl.BlockSpec((tm, width), lambda i: (i, 0))
    consts = [g[None, :]] + parts + [e64, qg, kg]
    out_specs = [row(S5_WIDTH), row(DA_WIDTH), row(DA_WIDTH), pl.BlockSpec((1, DA_WIDTH, tm), lambda i: (i, 0, 0)),
                 row(RW_SHIFT_WIDTH), row(N_BRANCH * D_MODEL)]
    out_shape = [jax.ShapeDtypeStruct((t, S5_WIDTH), F32), jax.ShapeDtypeStruct((t, DA_WIDTH), BF16),
                 jax.ShapeDtypeStruct((t, DA_WIDTH), BF16), jax.ShapeDtypeStruct((t // tm, DA_WIDTH, tm), BF16),
                 jax.ShapeDtypeStruct((t, RW_SHIFT_WIDTH), F32), jax.ShapeDtypeStruct((t, N_BRANCH * D_MODEL), F32)]
    return pl.pallas_call(
        _in_proj_kernel,
        grid=(t // tm,),
        in_specs=[row(D_MODEL)] + [_const_spec(c.shape) for c in consts],
        out_specs=out_specs,
        out_shape=out_shape,
        compiler_params=_cparams(1),
        name="in_proj",
    )(x2, *consts)


def _s5_tables(lam_re, lam_im, log_dt, b_re, b_im, c_re, c_im, n_double):
    n = S5_CHUNK
    dt = jnp.exp(log_dt)[:, None]
    den = lam_re * lam_re + lam_im * lam_im

    def a_pow(tau):
        mag = jnp.exp(lam_re * dt * tau)
        return mag * jnp.cos(lam_im * dt * tau), mag * jnp.sin(lam_im * dt * tau)

    ar, ai = a_pow(1.0)
    nr = ar - 1.0
    kr = (nr * lam_re + ai * lam_im) / den
    ki = (ai * lam_re - nr * lam_im) / den
    bbar_r = kr[..., None] * b_re - ki[..., None] * b_im
    bbar_i = kr[..., None] * b_im + ki[..., None] * b_re
    taus = jnp.arange(n + 1, dtype=lam_re.dtype)
    pr, pi = jax.vmap(a_pow)(taus)
    hp = lax.Precision.HIGHEST
    abr = pr[..., None] * bbar_r[None] - pi[..., None] * bbar_i[None]
    abi = pr[..., None] * bbar_i[None] + pi[..., None] * bbar_r[None]
    ktau = (jnp.einsum('gdp,ngpc->ngdc', c_re, abr, precision=hp)
            - jnp.einsum('gdp,ngpc->ngdc', c_im, abi, precision=hp))
    g, q, c, p2 = lam_re.shape[0], S5_SLAB, S5_GROUP, 2 * S5_STATE
    j = g // q
    eye = jnp.eye(q, dtype=ktau.dtype)
    kl = ktau[:n].transpose(1, 0, 3, 2).reshape(j, q, n, c, c).transpose(0, 2, 1, 3, 4)
    klag = (kl[:, :, :, :, None, :] * eye[None, None, :, None, :, None]).reshape(j, n, q * c, q * c)
    rev = n - 1 - jnp.arange(n)
    b_end = jnp.concatenate([abr[rev], abi[rev]], axis=2)
    bend = b_end.reshape(n, j, q, p2, c).transpose(1, 0, 2, 4, 3).reshape(j, n, q * c, p2)
    cr = c_re[None] * pr[1:, :, None, :] - c_im[None] * pi[1:, :, None, :]
    ci = c_re[None] * pi[1:, :, None, :] + c_im[None] * pr[1:, :, None, :]
    c_in = jnp.concatenate([cr, -ci], axis=3)
    cin = c_in.reshape(n, j, q, c, p2).transpose(1, 0, 4, 2, 3).reshape(j, n, p2, q * c)
    rows = []
    for k in range(n_double):
        sr, si = a_pow(float(n * 2 ** k))
        rows += [jnp.concatenate([sr, sr], axis=1), jnp.concatenate([-si, si], axis=1)]
    step = jnp.stack(rows, axis=1)
    step = step.reshape(j, q, 2 * n_double, p2).transpose(0, 2, 1, 3).reshape(j, 2 * n_double, q * p2)
    return klag.astype(BF16), bend.astype(BF16), cin.astype(BF16), step


def _s5_kernel(u_ref, klag_ref, bend_ref, cin_ref, st_ref, y_ref, tt_sc, bb_sc, cc_sc, *, n_double):
    n, w = S5_CHUNK, S5_SLAB * S5_GROUP

    @pl.when(pl.program_id(1) == 0)
    def _():
        zero = jnp.zeros((w, w), BF16)
        row_q = lax.broadcasted_iota(jnp.int32, (w, w), 0) // S5_GROUP
        lane_q = lax.broadcasted_iota(jnp.int32, (w, w), 1) // S5_GROUP
        for s in range(n):
            for t in range(n):
                tt_sc[s * w:(s + 1) * w, t * w:(t + 1) * w] = klag_ref[0, t - s] if t >= s else zero
            b_blk, c_blk = bend_ref[0, s], cin_ref[0, s]
            for q in range(S5_SLAB):
                bb_sc[s * w:(s + 1) * w, q * w:(q + 1) * w] = jnp.where(row_q == q, b_blk, zero)
                cc_sc[q * w:(q + 1) * w, s * w:(s + 1) * w] = jnp.where(lane_q == q, c_blk, zero)

    u = jnp.concatenate([u_ref[:, s, :].astype(BF16) for s in range(n)], axis=1)
    x = jnp.dot(u, bb_sc[...], preferred_element_type=F32)
    row = lax.broadcasted_iota(jnp.int32, x.shape, 0)
    is_re = lax.broadcasted_iota(jnp.int32, x.shape, 1) % (2 * S5_STATE) < S5_STATE
    width = x.shape[1]
    for k in range(n_double):
        sh = jnp.where(row >= 2 ** k, pltpu.roll(x, 2 ** k, 0), 0.0)
        swapped = jnp.where(is_re, pltpu.roll(sh, width - S5_STATE, 1), pltpu.roll(sh, S5_STATE, 1))
        x = x + st_ref[0, 2 * k:2 * k + 1, :] * sh + st_ref[0, 2 * k + 1:2 * k + 2, :] * swapped
    xin = jnp.where(row >= 1, pltpu.roll(x, 1, 0), 0.0).astype(BF16)
    y = jnp.dot(u, tt_sc[...], preferred_element_type=F32) + jnp.dot(xin, cc_sc[...], preferred_element_type=F32)
    for t in range(n):
        y_ref[:, t, :] = y[:, t * w:(t + 1) * w]


def _s5_scan(u, tables, n_batch, seq):
    klag, bend, cin, st = tables
    n_chunk = seq // S5_CHUNK
    n_slab = klag.shape[0]
    w = S5_SLAB * S5_GROUP
    u3 = u.reshape(n_batch * n_chunk, S5_CHUNK, S5_WIDTH)
    per_slab = lambda a: pl.BlockSpec((1,) + a.shape[1:], lambda j, b: (j,) + (0,) * (a.ndim - 1))
    io_spec = pl.BlockSpec((n_chunk, S5_CHUNK, w), lambda j, b: (b, 0, j))
    y = pl.pallas_call(
        functools.partial(_s5_kernel, n_double=st.shape[1] // 2),
        grid=(n_slab, n_batch),
        in_specs=[io_spec, per_slab(klag), per_slab(bend), per_slab(cin), per_slab(st)],
        out_specs=io_spec,
        out_shape=jax.ShapeDtypeStruct(u3.shape, F32),
        scratch_shapes=[pltpu.VMEM((S5_CHUNK * w, S5_CHUNK * w), BF16),
                        pltpu.VMEM((S5_CHUNK * w, S5_SLAB * 2 * S5_STATE), BF16),
                        pltpu.VMEM((S5_SLAB * 2 * S5_STATE, S5_CHUNK * w), BF16)],
        compiler_params=_cparams(2),
        name="s5_chunk",
    )(u3, klag, bend, cin, st)
    return y.reshape(u.shape)


def _attn_kernel(lam_ref, sg_ref, q_ref, k_ref, vt_ref, o_ref, m_sc, acc_sc, *, tq, lambda_init):
    tk = vt_ref.shape[2]
    qi = pl.program_id(2)
    q = q_ref[...]
    lane = lax.broadcasted_iota(jnp.int32, q.shape, 1)
    zero = jnp.zeros_like(q)
    q2 = jnp.concatenate([jnp.where(lane < DA_HEAD_DIM, q, zero), jnp.where(lane >= DA_HEAD_DIM, q, zero)], axis=0)
    m_sc[...] = jnp.full(m_sc.shape, NEG_BIG, F32)
    acc_sc[...] = jnp.zeros(acc_sc.shape, F32)

    n_strip = 2 * tq // ATTN_STRIP
    n_sub = tq // tk
    ones_rows = jnp.ones((16, tk), BF16)

    def step(j, masked):
        kb = k_ref[pl.ds(pl.multiple_of(j * tq, tq), tq), :]
        vtb = [jnp.concatenate([vt_ref[j * n_sub + i], ones_rows], axis=0) for i in range(n_sub)]
        s = [lax.dot_general(kb, q2[c * ATTN_STRIP:(c + 1) * ATTN_STRIP], (((1,), (1,)), ((), ())),
                             preferred_element_type=F32) for c in range(n_strip)]
        for c in range(n_strip):
            cs = slice(c * ATTN_STRIP, (c + 1) * ATTN_STRIP)
            sc = s[c]
            if masked:
                key = lax.broadcasted_iota(jnp.int32, sc.shape, 0)
                col = lax.broadcasted_iota(jnp.int32, sc.shape, 1)
                sc = jnp.where(key <= col + (c * ATTN_STRIP) % tq, sc, NEG_BIG)
            m_prev = m_sc[:, cs]
            m_next = jnp.maximum(m_prev, jnp.max(sc, axis=0, keepdims=True))
            alpha = jnp.exp2(m_prev - m_next)
            p = jnp.exp2(sc - m_next).astype(BF16)
            m_sc[:, cs] = m_next
            pv = sum(jnp.dot(vtb[i], p[i * tk:(i + 1) * tk], preferred_element_type=F32) for i in range(n_sub))
            acc_sc[:, cs] = alpha * acc_sc[:, cs] + pv

    def body(jj, carry):
        step(2 * jj, False)
        step(2 * jj + 1, False)
        return carry

    lax.fori_loop(0, qi // 2, body, 0)

    @pl.when(qi % 2 == 1)
    def _():
        step(qi - 1, False)

    step(qi, True)

    lv = lam_ref[...]
    lam = (jnp.exp(jnp.sum(lv[0:1] * lv[1:2], axis=1, keepdims=True))
           - jnp.exp(jnp.sum(lv[2:3] * lv[3:4], axis=1, keepdims=True)) + lambda_init)
    o_all = acc_sc[:DA_V_DIM, :] / acc_sc[DA_V_DIM:DA_V_DIM + 1, :]
    o = (o_all[:, :tq] - lam * o_all[:, tq:]).T
    o = o * lax.rsqrt(jnp.mean(o * o, axis=-1, keepdims=True) + DA_SUBLN_EPS) * sg_ref[...] * (1.0 - lambda_init)
    o_ref[...] = o.astype(o_ref.dtype)


def _diff_attn(q, k, vt, lam_vecs, subln_g, lambda_init, n_batch, seq):
    tk = vt.shape[2]
    tq = min(ATTN_TQ, seq)
    nq = seq // tq
    q_spec = pl.BlockSpec((tq, DA_V_DIM), lambda b, h, i: (b * nq + i, h))
    k_spec = pl.BlockSpec((seq, DA_V_DIM), lambda b, h, i: (b, h))
    vt_spec = pl.BlockSpec((seq // tk, DA_V_DIM, tk), lambda b, h, i: (b, h, 0))
    return pl.pallas_call(
        functools.partial(_attn_kernel, tq=tq, lambda_init=lambda_init),
        grid=(n_batch, DA_HEADS, nq),
        in_specs=[_const_spec(lam_vecs.shape), _const_spec((1, DA_V_DIM)), q_spec, k_spec, vt_spec],
        out_specs=q_spec,
        out_shape=jax.ShapeDtypeStruct((n_batch * seq, DA_WIDTH), BF16),
        scratch_shapes=[pltpu.VMEM((1, 2 * tq), F32), pltpu.VMEM((DA_V_DIM + 16, 2 * tq), F32)],
        compiler_params=_cparams(3),
        name="diff_attn",
    )(lam_vecs, subln_g[None, :], q, k, vt)


def _rw_prep_kernel(*refs, tm, seq, gated):
    if gated:
        (z_ref, zp_ref, vf_ref, mu_ref, w0_ref, a0_ref, kk_ref, ka_ref, rk_ref, w2h_ref, w2l_ref, a2h_ref, a2l_ref,
         g2h_ref, g2l_ref, e64_ref, tri_ref, v0_ref, v1h_ref, v1l_ref, v2h_ref, v2l_ref,
         rh_ref, ah_ref, bh_ref, kh_ref, bt_ref, kt_ref, v_ref, gc_ref, bonus_ref, g_ref) = refs
    else:
        (z_ref, zp_ref, mu_ref, w0_ref, a0_ref, kk_ref, ka_ref, rk_ref, w2h_ref, w2l_ref, a2h_ref, a2l_ref,
         g2h_ref, g2l_ref, e64_ref, tri_ref,
         rh_ref, ah_ref, bh_ref, kh_ref, bt_ref, kt_ref, v_ref, gc_ref, bonus_ref, g_ref, vfirst_ref) = refs
    i = pl.program_id(0)
    z = z_ref[...]
    first = jnp.where((i * tm) % seq == 0, 0.0, 1.0) * zp_ref[7:8, :]
    row = lax.broadcasted_iota(jnp.int32, z.shape, 0)
    prev = jnp.where(row == 0, first, pltpu.roll(z, 1, 0))
    zf = z + (prev - z) * mu_ref[...]
    r = zf[:, :RW_WIDTH]
    k = zf[:, RW_WIDTH:2 * RW_WIDTH]
    v = zf[:, 2 * RW_WIDTH:3 * RW_WIDTH]
    tail = zf[:, 3 * RW_WIDTH:]
    w_in = w0_ref[...] + _mm_split(jnp.tanh(tail), w2h_ref[...], w2l_ref[...])
    softplus = jnp.maximum(-w_in, 0.0) + jnp.log(1.0 + jnp.exp(-jnp.abs(w_in)))
    logw = -jnp.exp(-softplus - 0.5)
    a = _sigmoid(a0_ref[...] + _mm_split(tail, a2h_ref[...], a2l_ref[...]))
    g_ref[...] = _mm_split(_sigmoid(tail), g2h_ref[...], g2l_ref[...])
    kk = k * kk_ref[...]
    norm = jnp.sqrt(_mm_sel_right(kk * kk, e64_ref[...]))
    kk = kk / jnp.maximum(norm, 1e-12)
    k = k * (1.0 + (a - 1.0) * ka_ref[...])
    if gated:
        lo = _mm_split(v, v1h_ref[...], v1l_ref[...])
        mix = _sigmoid(v0_ref[...] + _mm_split(lo, v2h_ref[...], v2l_ref[...]))
        v = v + (vf_ref[...] - v) * mix
    else:
        vfirst_ref[...] = v
    bonus_ref[...] = _mm_sel_right(r * k * rk_ref[...], e64_ref[...]) * v
    cs = _mm_sel_left(tri_ref[...], logw)
    cs_end = jnp.concatenate(
        [jnp.broadcast_to(cs[c * RW_CHUNK + RW_CHUNK - 1:(c + 1) * RW_CHUNK, :], (RW_CHUNK, RW_WIDTH))
         for c in range(tm // RW_CHUNK)], axis=0)
    inv = jnp.exp(-cs)
    to_end = jnp.exp(cs_end - cs)
    b = kk * a
    rh_ref[...] = (r * jnp.exp(cs)).astype(BF16)
    ah_ref[...] = (-kk * jnp.exp(cs - logw)).astype(BF16)
    bh_ref[...] = (b * inv).astype(BF16)
    kh_ref[...] = (k * inv).astype(BF16)
    bt_ref[...] = (b * to_end).astype(BF16)
    kt_ref[...] = (k * to_end).astype(BF16)
    v_ref[...] = v.astype(BF16)
    gc_ref[...] = jnp.exp(cs_end)


def _pad_rows(w, start):
    return jnp.zeros((RW_TAIL, w.shape[1]), w.dtype).at[start:start + w.shape[0]].set(w)


def _hi_lo(w):
    hi = w.astype(BF16)
    return hi, (w - hi.astype(w.dtype)).astype(BF16)


def _rw_prep(z, p, v_first, seq, tm):
    t = z.shape[0]
    gated = v_first is not None
    idx = jnp.arange(tm)
    same_chunk = (idx[:, None] // RW_CHUNK) == (idx[None, :] // RW_CHUNK)
    tri = (same_chunk & (idx[None, :] <= idx[:, None])).astype(BF16)
    vec = lambda a: a[None, :]
    consts = [vec(p['mu']), vec(p['w0']), vec(p['a0']), vec(p['k_k']), vec(p['k_a']), vec(p['r_k'].reshape(-1)),
              *_hi_lo(_pad_rows(p['w2'], 0)), *_hi_lo(_pad_rows(p['a2'], RW_DECAY_LORA)),
              *_hi_lo(_pad_rows(p['g2'], RW_DECAY_LORA + RW_A_LORA)),
              _group_ones(RW_WIDTH, RW_HEAD), tri]
    if gated:
        consts += [vec(p['v0']), *_hi_lo(p['v1']), *_hi_lo(p['v2'])]
    row = lambda width: pl.BlockSpec((tm, width), lambda i: (i, 0))
    zp_spec = pl.BlockSpec((8, RW_SHIFT_WIDTH), lambda i: (jnp.maximum(i * (tm // 8) - 1, 0), 0))
    in_specs = [row(RW_SHIFT_WIDTH), zp_spec] + ([row(RW_WIDTH)] if gated else []) + [_const_spec(c.shape) for c in consts]
    args = [z, z] + ([v_first] if gated else []) + consts
    n_out = 10 if gated else 11
    dtypes = [BF16] * 7 + [F32] * 3 + ([] if gated else [F32])
    outs = pl.pallas_call(
        functools.partial(_rw_prep_kernel, tm=tm, seq=seq, gated=gated),
        grid=(t // tm,),
        in_specs=in_specs,
        out_specs=[row(RW_WIDTH)] * n_out,
        out_shape=[jax.ShapeDtypeStruct((t, RW_WIDTH), dt) for dt in dtypes],
        compiler_params=_cparams(1),
        name="rw_prep",
    )(*args)
    return outs


def _rw_intra_kernel(rh_ref, ah_ref, bh_ref, kh_ref, bt_ref, kt_ref, v_ref, gc_ref,
                     wr_ref, mb_ref, uloc_ref, oloc_ref, kvloc_ref, gcol_ref, *, n_sub):
    n = RW_CHUNK
    ri = lax.broadcasted_iota(jnp.int32, (n, n), 0)
    ci = lax.broadcasted_iota(jnp.int32, (n, n), 1)
    strict = ci < ri
    incl = ci <= ri
    eye = ci == ri
    eye_b = eye.astype(BF16)
    ones_b = jnp.ones((n, n), BF16)
    items = [(c, h) for c in range(n_sub) for h in range(RW_HEADS)]
    rows = lambda c: slice(c * n, (c + 1) * n)
    lanes = lambda h: slice(h * RW_HEAD, (h + 1) * RW_HEAD)
    ld = lambda ref: [ref[rows(c), lanes(h)] for c, h in items]
    rh, ah, bh, kh, bt, kt, v = (ld(r) for r in (rh_ref, ah_ref, bh_ref, kh_ref, bt_ref, kt_ref, v_ref))
    ar = [jnp.concatenate([a, r], axis=0) for a, r in zip(ah, rh)]
    gb = [_mm_nt(x, y) for x, y in zip(ar, bh)]
    gk = [_mm_nt(x, y) for x, y in zip(ar, kh)]
    l_ab = [jnp.where(strict, g[:n], 0.0) for g in gb]
    m_rb = [jnp.where(incl, g[n:], 0.0).astype(BF16) for g in gb]
    lm_k = [jnp.concatenate([jnp.where(strict, g[:n], 0.0), jnp.where(incl, g[n:], 0.0)], axis=0).astype(BF16) for g in gk]
    akv = [_mm(x, y) for x, y in zip(lm_k, v)]
    tinv = [jnp.where(eye, 1.0, x) for x in l_ab]
    l_ab = [x.astype(BF16) for x in l_ab]
    pw = [_mm(x, x) for x in l_ab]
    for _ in range(4):
        pw_b = [x.astype(BF16) for x in pw]
        res = [_mm(jnp.concatenate([p, t.astype(BF16)], axis=0), p) for p, t in zip(pw_b, tinv)]
        pw = [r[:n] for r in res]
        tinv = [t + r[n:] for t, r in zip(tinv, res)]
    tinv = [(t + _mm(t, p)).astype(BF16) for t, p in zip(tinv, pw)]
    w = [_mm(t, a) for t, a in zip(tinv, ah)]
    uloc = [_mm(t, x[:n]) for t, x in zip(tinv, akv)]
    kvloc = [_mm_tn(x, y) for x, y in zip(kt, v)]
    bt_t = [_mm_tn(x, eye_b) for x in bt]
    for i, (c, h) in enumerate(items):
        decay = jnp.where(eye, jnp.broadcast_to(gc_ref[(c + 1) * n - 1:(c + 1) * n, lanes(h)], (n, n)), 0.0)
        gcol_ref[rows(c), lanes(h)] = _mm_sel_right(decay, ones_b, terms=3)
        wr_ref[2 * c * n:(2 * c + 1) * n, lanes(h)] = w[i].astype(BF16)
        wr_ref[(2 * c + 1) * n:(2 * c + 2) * n, lanes(h)] = rh[i]
        mb_ref[2 * c * n:(2 * c + 1) * n, lanes(h)] = m_rb[i]
        mb_ref[(2 * c + 1) * n:(2 * c + 2) * n, lanes(h)] = bt_t[i].astype(BF16)
        uloc_ref[rows(c), lanes(h)] = uloc[i]
        oloc_ref[rows(c), lanes(h)] = akv[i][n:]
        kvloc_ref[rows(c), lanes(h)] = kvloc[i]


def _rw_seq_kernel(wr_ref, mb_ref, uloc_ref, oloc_ref, kvloc_ref, gcol_ref, y_ref, h_sc, *, n_batch):
    @pl.when(pl.program_id(0) == 0)
    def _():
        h_sc[...] = jnp.zeros(h_sc.shape, F32)

    n = RW_CHUNK
    blk = (lax.broadcasted_iota(jnp.int32, (RW_WIDTH, RW_WIDTH), 0) // RW_HEAD
           == lax.broadcasted_iota(jnp.int32, (RW_WIDTH, RW_WIDTH), 1) // RW_HEAD)
    block_diag = lambda x: jnp.where(blk, jnp.concatenate([x] * RW_HEADS, axis=0), 0.0).astype(BF16)
    batches = range(n_batch)
    h0 = [h_sc[b] for b in batches]
    s1 = [jnp.dot(wr_ref[b], block_diag(h0[b]), preferred_element_type=F32) for b in batches]
    u = [s1[b][:n] + uloc_ref[b] for b in batches]
    s2 = [jnp.dot(mb_ref[b], block_diag(u[b]), preferred_element_type=F32) for b in batches]
    for b in batches:
        y_ref[b] = s1[b][n:] + s2[b][:n] + oloc_ref[b]
        h_sc[b] = gcol_ref[b] * h0[b] + s2[b][n:] + kvloc_ref[b]


def _rw_scan(ops, n_batch, seq):
    t = n_batch * seq
    n_sub = 4
    rows = n_sub * RW_CHUNK
    spec = pl.BlockSpec((rows, RW_WIDTH), lambda i: (i, 0))
    spec2 = pl.BlockSpec((2 * rows, RW_WIDTH), lambda i: (i, 0))
    wr, mb, uloc, oloc, kvloc, gcol = pl.pallas_call(
        functools.partial(_rw_intra_kernel, n_sub=n_sub),
        grid=(t // rows,),
        in_specs=[spec] * 8,
        out_specs=[spec2, spec2, spec, spec, spec, spec],
        out_shape=[jax.ShapeDtypeStruct((2 * t, RW_WIDTH), BF16)] * 2 + [jax.ShapeDtypeStruct((t, RW_WIDTH), F32)] * 4,
        compiler_params=_cparams(1),
        name="rw_intra",
    )(*ops)
    n_chunk = seq // RW_CHUNK
    b3 = lambda a: a.reshape(n_batch, -1, RW_WIDTH)
    cspec = pl.BlockSpec((n_batch, RW_CHUNK, RW_WIDTH), lambda c: (0, c, 0))
    cspec2 = pl.BlockSpec((n_batch, 2 * RW_CHUNK, RW_WIDTH), lambda c: (0, c, 0))
    y = pl.pallas_call(
        functools.partial(_rw_seq_kernel, n_batch=n_batch),
        grid=(n_chunk,),
        in_specs=[cspec2, cspec2, cspec, cspec, cspec, cspec],
        out_specs=cspec,
        out_shape=jax.ShapeDtypeStruct((n_batch, seq, RW_WIDTH), F32),
        scratch_shapes=[pltpu.VMEM((n_batch, RW_HEAD, RW_WIDTH), F32)],
        compiler_params=_cparams(1),
        name="rw_seq",
    )(b3(wr), b3(mb), b3(uloc), b3(oloc), b3(kvloc), b3(gcol))
    return y.reshape(t, RW_WIDTH)


def _merge_kernel(x_ref, ys5_ref, us5_ref, yb_ref, yrw_ref, bonus_ref, g_ref, gl_ref, d_ref, wglu_ref, lng_ref, lnb_ref,
                  e64_ref, wb_ref, wout_ref, o_ref):
    ya = ys5_ref[...] + d_ref[...] * us5_ref[...]
    ya = 0.5 * ya * (1.0 + jnp.tanh(math.sqrt(2.0 / math.pi) * (ya + 0.044715 * (ya * ya * ya))))
    ya = ya * _sigmoid(_mm(ya, wglu_ref[...]))
    y = yrw_ref[...]
    mean = _mm_sel_right(y, e64_ref[...]) * (1.0 / RW_HEAD)
    yc = y - mean
    var = _mm_sel_right(yc * yc, e64_ref[...]) * (1.0 / RW_HEAD)
    yc = yc * lax.rsqrt(var + RW_LN_EPS) * lng_ref[...] + lnb_ref[...]
    yc = (yc + bonus_ref[...]) * g_ref[...]
    merged = jnp.zeros((x_ref.shape[0], D_MODEL), F32)
    for n, br in enumerate((ya, yb_ref[...], yc)):
        gate = _sigmoid(gl_ref[:, n * D_MODEL:(n + 1) * D_MODEL])
        merged = merged + gate * _mm(br, wb_ref[n])
    o_ref[...] = x_ref[...] + _mm(merged, wout_ref[...])


def _merge(x2, ys5, us5, yb, yrw, bonus, g, gl, s5_d, w_glu, ln_g, ln_b, w_branch, w_out, tm):
    t = x2.shape[0]
    consts = [s5_d[None, :], w_glu.astype(BF16), ln_g[None, :], ln_b[None, :], _group_ones(RW_WIDTH, RW_HEAD),
              w_branch.astype(BF16), w_out.astype(BF16)]
    row = lambda width: pl.BlockSpec((tm, width), lambda i: (i, 0))
    widths = [D_MODEL, S5_WIDTH, S5_WIDTH, DA_WIDTH, RW_WIDTH, RW_WIDTH, RW_WIDTH, N_BRANCH * D_MODEL]
    return pl.pallas_call(
        _merge_kernel,
        grid=(t // tm,),
        in_specs=[row(wd) for wd in widths] + [_const_spec(c.shape) for c in consts],
        out_specs=row(D_MODEL),
        out_shape=jax.ShapeDtypeStruct((t, D_MODEL), F32),
        compiler_params=_cparams(1),
        name="merge",
    )(x2, ys5, us5, yb, yrw, bonus, g, gl, *consts)


def _ffn_kernel(x_ref, g_ref, win_ref, wout_ref, o_ref):
    x = x_ref[...]
    h = (x * lax.rsqrt(jnp.mean(x * x, axis=-1, keepdims=True) + NORM_EPS) * g_ref[...]).astype(BF16)
    gate = jnp.dot(h, win_ref[:, :D_FF], preferred_element_type=F32)
    up = jnp.dot(h, win_ref[:, D_FF:], preferred_element_type=F32)
    act = gate * _sigmoid(gate) * up
    o_ref[...] = x + _mm(act, wout_ref[...])


def _ffn(x2, g, w_in, w_out, tm):
    t = x2.shape[0]
    consts = [g[None, :], w_in.astype(BF16), w_out.astype(BF16)]
    row = pl.BlockSpec((tm, D_MODEL), lambda i: (i, 0))
    return pl.pallas_call(
        _ffn_kernel,
        grid=(t // tm,),
        in_specs=[row] + [_const_spec(c.shape) for c in consts],
        out_specs=row,
        out_shape=jax.ShapeDtypeStruct((t, D_MODEL), F32),
        compiler_params=_cparams(1),
        name="ffn",
    )(x2, *consts)


def _layer(x2, i, n_batch, seq, p, v_first, tm):
    us5, q, k, v, zrw, gl = _in_proj(x2, p['norm1_g'], p['w_in'], p['da_q_gain'], p['da_k_gain'], tm)
    n_double = max(1, (seq // S5_CHUNK - 1).bit_length())
    tables = _s5_tables(p['s5_lambda_re'], p['s5_lambda_im'], p['s5_log_dt'], p['s5_b_re'], p['s5_b_im'],
                        p['s5_c_re'], p['s5_c_im'], n_double)
    ys5 = _s5_scan(us5, tables, n_batch, seq)
    lambda_init = 0.8 - 0.6 * math.exp(-0.3 * i)
    yb = _diff_attn(q, k, v, p['da_lambda'], p['da_subln_g'], lambda_init, n_batch, seq)
    rw = {name[3:]: val for name, val in p.items() if name.startswith('rw_')}
    outs = _rw_prep(zrw, rw, v_first, seq, min(RW_PREP_TILE, seq))
    if v_first is None:
        v_first = outs[10]
    yrw = _rw_scan(outs[:8], n_batch, seq)
    x2 = _merge(x2, ys5, us5, yb, yrw, outs[8], outs[9], gl, p['s5_d'], p['s5_w_glu'], p['rw_ln_g'], p['rw_ln_b'],
                p['w_branch'], p['w_out'], tm)
    x2 = _ffn(x2, p['norm2_g'], p['w_ffn_in'], p['w_ffn_out'], tm)
    return x2, v_first


def kernel(x, norm1_g, w_in, s5_lambda_re, s5_lambda_im, s5_log_dt, s5_b_re, s5_b_im, s5_c_re, s5_c_im, s5_d, s5_w_glu, da_q_gain, da_k_gain, da_lambda, da_subln_g, rw_mu, rw_w0, rw_w2, rw_a0, rw_a2, rw_g2, rw_k_k, rw_k_a, rw_r_k, rw_ln_g, rw_ln_b, rw_v0, rw_v1, rw_v2, w_branch, w_out, norm2_g, w_ffn_in, w_ffn_out):
    n_batch, seq, _ = x.shape
    per_layer = dict(norm1_g=norm1_g, w_in=w_in, s5_lambda_re=s5_lambda_re, s5_lambda_im=s5_lambda_im,
                     s5_log_dt=s5_log_dt, s5_b_re=s5_b_re, s5_b_im=s5_b_im, s5_c_re=s5_c_re, s5_c_im=s5_c_im,
                     s5_d=s5_d, s5_w_glu=s5_w_glu, da_q_gain=da_q_gain, da_k_gain=da_k_gain, da_lambda=da_lambda,
                     da_subln_g=da_subln_g, rw_mu=rw_mu, rw_w0=rw_w0, rw_w2=rw_w2, rw_a0=rw_a0, rw_a2=rw_a2,
                     rw_g2=rw_g2, rw_k_k=rw_k_k, rw_k_a=rw_k_a, rw_r_k=rw_r_k, rw_ln_g=rw_ln_g, rw_ln_b=rw_ln_b,
                     w_branch=w_branch, w_out=w_out, norm2_g=norm2_g, w_ffn_in=w_ffn_in, w_ffn_out=w_ffn_out)
    tm = min(TOKEN_TILE, seq)
    x2 = x.reshape(n_batch * seq, D_MODEL)
    v_first = None
    for i in range(w_in.shape[0]):
        p = {name: val[i] for name, val in per_layer.items()}
        if i > 0:
            p.update(rw_v0=rw_v0[i - 1], rw_v1=rw_v1[i - 1], rw_v2=rw_v2[i - 1])
        x2, v_first = _layer(x2, i, n_batch, seq, p, v_first, tm)
    return x2.reshape(x.shape)
```

```python
import functools
import math

import jax
import jax.numpy as jnp
from jax import lax
from jax.experimental import pallas as pl
from jax.experimental.pallas import tpu as pltpu

F32 = jnp.float32
BF16 = jnp.bfloat16

D_MODEL = 1024
DEPTH = 2
S5_WIDTH = 512
S5_GROUP = 16
S5_GROUPS = S5_WIDTH // S5_GROUP
S5_STATE = 64
S5_CHUNK = 16
S5_SLAB = 8
DA_HEADS = 4
DA_HEAD_DIM = 64
DA_V_DIM = 2 * DA_HEAD_DIM
DA_WIDTH = DA_HEADS * DA_V_DIM
DA_SUBLN_EPS = 1e-5
RW_HEAD = 64
RW_WIDTH = 512
RW_HEADS = RW_WIDTH // RW_HEAD
RW_DECAY_LORA = 32
RW_A_LORA = 32
RW_G_LORA = 96
RW_TAIL = RW_DECAY_LORA + RW_A_LORA + RW_G_LORA
RW_SHIFT_WIDTH = 3 * RW_WIDTH + RW_TAIL
RW_LN_EPS = 64e-5
RW_CHUNK = 64
N_BRANCH = 3
D_FF = 2816
NORM_EPS = 1e-6
NEG_BIG = -1e30
ATTN_TQ = 512
ATTN_STRIP = 256
TOKEN_TILE = 512
GROUP_SUM_TERMS = 1
RW_PREP_TILE = 256

VMEM_LIMIT = 56 * 1024 * 1024


def _cparams(n_axes):
    return pltpu.CompilerParams(dimension_semantics=("arbitrary",) * n_axes, vmem_limit_bytes=VMEM_LIMIT)


def _const_spec(shape):
    nd = len(shape)
    return pl.BlockSpec(shape, lambda *_: (0,) * nd, pipeline_mode=pl.Buffered(1))


def _mm(a, b):
    return jnp.dot(a.astype(BF16), b.astype(BF16), preferred_element_type=F32)


def _mm_nt(a, b):
    return lax.dot_general(a.astype(BF16), b.astype(BF16), (((1,), (1,)), ((), ())), preferred_element_type=F32)


def _mm_tn(a, b):
    return lax.dot_general(a.astype(BF16), b.astype(BF16), (((0,), (0,)), ((), ())), preferred_element_type=F32)


def _split2(x):
    hi = x.astype(BF16)
    lo = (x - hi.astype(F32)).astype(BF16)
    return hi, lo


def _split3(x):
    hi = x.astype(BF16)
    r = x - hi.astype(F32)
    mid = r.astype(BF16)
    lo = (r - mid.astype(F32)).astype(BF16)
    return hi, mid, lo


def _bf16_terms(x, terms):
    return {1: lambda: (x.astype(BF16),), 2: lambda: _split2(x), 3: lambda: _split3(x)}[terms]()


def _mm_sel_right(x, sel, terms):
    return sum(jnp.dot(p, sel, preferred_element_type=F32) for p in _bf16_terms(x, terms))


def _mm_sel_left(sel, x, terms):
    return sum(jnp.dot(sel, p, preferred_element_type=F32) for p in _bf16_terms(x, terms))


def _mm_split(x, w_hi, w_lo):
    x_hi, x_lo = _split2(x)
    d = lambda p, q: jnp.dot(p, q, preferred_element_type=F32)
    return d(x_hi, w_hi) + d(x_hi, w_lo) + d(x_lo, w_hi)


def _sigmoid(x):
    return 1.0 / (1.0 + jnp.exp(-x))


def _group_ones(width, group):
    r = jnp.arange(width) // group
    return (r[:, None] == r[None, :]).astype(BF16)


def _in_proj_kernel(x_ref, g_ref, ws5_ref, wq_ref, wk_ref, wvt_ref, wrw_ref, wgate_ref, e64_ref, qg_ref, kg_ref,
                    us5_ref, q_ref, k_ref, vt_ref, zrw_ref, gl_ref):
    x = x_ref[...]
    h = x * lax.rsqrt(jnp.mean(x * x, axis=-1, keepdims=True) + NORM_EPS) * g_ref[...]
    h = h.astype(BF16)
    dot = lambda w_ref: jnp.dot(h, w_ref[...], preferred_element_type=F32)
    us5_ref[...] = dot(ws5_ref)

    def qk_norm(t, gain_ref):
        ms = _mm_sel_right(t * t, e64_ref[...], GROUP_SUM_TERMS) * (1.0 / DA_HEAD_DIM)
        return t * lax.rsqrt(ms + NORM_EPS) * gain_ref[...]

    q_ref[...] = qk_norm(dot(wq_ref), qg_ref).astype(BF16)
    k_ref[...] = qk_norm(dot(wk_ref), kg_ref).astype(BF16)
    vt_ref[0] = lax.dot_general(wvt_ref[...], h, (((1,), (1,)), ((), ())), preferred_element_type=F32).astype(BF16)
    zrw_ref[...] = dot(wrw_ref)
    gl_ref[...] = dot(wgate_ref)


def _in_proj(x2, g, w, q_gain, k_gain, tm):
    t = x2.shape[0]
    w = w.astype(BF16)
    o = 0
    parts = []
    for width in (S5_WIDTH, DA_WIDTH, DA_WIDTH, DA_WIDTH, RW_SHIFT_WIDTH, N_BRANCH * D_MODEL):
        parts.append(w[:, o:o + width])
        o += width
    parts[3] = parts[3].T
    e64 = _group_ones(DA_WIDTH, DA_HEAD_DIM)
    qg = jnp.tile(q_gain.reshape(-1), DA_HEADS)[None, :] * (DA_HEAD_DIM ** -0.5 * math.log2(math.e))
    kg = jnp.tile(k_gain.reshape(-1), DA_HEADS)[None, :]
    row = lambda width: pl.BlockSpec((tm, width), lambda i: (i, 0))
    consts = [g[None, :]] + parts + [e64, qg, kg]
    out_specs = [row(S5_WIDTH), row(DA_WIDTH), row(DA_WIDTH), pl.BlockSpec((1, DA_WIDTH, tm), lambda i: (i, 0, 0)),
                 row(RW_SHIFT_WIDTH), row(N_BRANCH * D_MODEL)]
    out_shape = [jax.ShapeDtypeStruct((t, S5_WIDTH), F32), jax.ShapeDtypeStruct((t, DA_WIDTH), BF16),
                 jax.ShapeDtypeStruct((t, DA_WIDTH), BF16), jax.ShapeDtypeStruct((t // tm, DA_WIDTH, tm), BF16),
                 jax.ShapeDtypeStruct((t, RW_SHIFT_WIDTH), F32), jax.ShapeDtypeStruct((t, N_BRANCH * D_MODEL), F32)]
    return pl.pallas_call(
        _in_proj_kernel,
        grid=(t // tm,),
        in_specs=[row(D_MODEL)] + [_const_spec(c.shape) for c in consts],
        out_specs=out_specs,
        out_shape=out_shape,
        compiler_params=_cparams(1),
        name="in_proj",
    )(x2, *consts)


def _s5_tables(lam_re, lam_im, log_dt, b_re, b_im, c_re, c_im, n_double):
    n = S5_CHUNK
    dt = jnp.exp(log_dt)[:, None]
    den = lam_re * lam_re + lam_im * lam_im

    def a_pow(tau):
        mag = jnp.exp(lam_re * dt * tau)
        return mag * jnp.cos(lam_im * dt * tau), mag * jnp.sin(lam_im * dt * tau)

    ar, ai = a_pow(1.0)
    nr = ar - 1.0
    kr = (nr * lam_re + ai * lam_im) / den
    ki = (ai * lam_re - nr * lam_im) / den
    bbar_r = kr[..., None] * b_re - ki[..., None] * b_im
    bbar_i = kr[..., None] * b_im + ki[..., None] * b_re
    taus = jnp.arange(n + 1, dtype=lam_re.dtype)
    pr, pi = jax.vmap(a_pow)(taus)
    hp = lax.Precision.HIGHEST
    abr = pr[..., None] * bbar_r[None] - pi[..., None] * bbar_i[None]
    abi = pr[..., None] * bbar_i[None] + pi[..., None] * bbar_r[None]
    ktau = (jnp.einsum('gdp,ngpc->ngdc', c_re, abr, precision=hp)
            - jnp.einsum('gdp,ngpc->ngdc', c_im, abi, precision=hp))
    g, q, c, p2 = lam_re.shape[0], S5_SLAB, S5_GROUP, 2 * S5_STATE
    j = g // q
    eye = jnp.eye(q, dtype=ktau.dtype)
    kl = ktau[:n].transpose(1, 0, 3, 2).reshape(j, q, n, c, c).transpose(0, 2, 1, 3, 4)
    klag = (kl[:, :, :, :, None, :] * eye[None, None, :, None, :, None]).reshape(j, n, q * c, q * c)
    rev = n - 1 - jnp.arange(n)
    b_end = jnp.concatenate([abr[rev], abi[rev]], axis=2)
    bend = b_end.reshape(n, j, q, p2, c).transpose(1, 0, 2, 4, 3).reshape(j, n, q * c, p2)
    cr = c_re[None] * pr[1:, :, None, :] - c_im[None] * pi[1:, :, None, :]
    ci = c_re[None] * pi[1:, :, None, :] + c_im[None] * pr[1:, :, None, :]
    c_in = jnp.concatenate([cr, -ci], axis=3)
    cin = c_in.reshape(n, j, q, c, p2).transpose(1, 0, 4, 2, 3).reshape(j, n, p2, q * c)
    rows = []
    for k in range(n_double):
        sr, si = a_pow(float(n * 2 ** k))
        rows += [jnp.concatenate([sr, sr], axis=1), jnp.concatenate([-si, si], axis=1)]
    step = jnp.stack(rows, axis=1)
    step = step.reshape(j, q, 2 * n_double, p2).transpose(0, 2, 1, 3).reshape(j, 2 * n_double, q * p2)
    return klag.astype(BF16), bend.astype(BF16), cin.astype(BF16), step


def _s5_kernel(u_ref, klag_ref, bend_ref, cin_ref, st_ref, y_ref, tt_sc, bb_sc, cc_sc, *, n_double):
    n, w = S5_CHUNK, S5_SLAB * S5_GROUP

    @pl.when(pl.program_id(1) == 0)
    def _():
        zero = jnp.zeros((w, w), BF16)
        row_q = lax.broadcasted_iota(jnp.int32, (w, w), 0) // S5_GROUP
        lane_q = lax.broadcasted_iota(jnp.int32, (w, w), 1) // S5_GROUP
        for s in range(n):
            for t in range(n):
                tt_sc[s * w:(s + 1) * w, t * w:(t + 1) * w] = klag_ref[0, t - s] if t >= s else zero
            b_blk, c_blk = bend_ref[0, s], cin_ref[0, s]
            for q in range(S5_SLAB):
                bb_sc[s * w:(s + 1) * w, q * w:(q + 1) * w] = jnp.where(row_q == q, b_blk, zero)
                cc_sc[q * w:(q + 1) * w, s * w:(s + 1) * w] = jnp.where(lane_q == q, c_blk, zero)

    u = jnp.concatenate([u_ref[:, s, :].astype(BF16) for s in range(n)], axis=1)
    x = jnp.dot(u, bb_sc[...], preferred_element_type=F32)
    row = lax.broadcasted_iota(jnp.int32, x.shape, 0)
    is_re = lax.broadcasted_iota(jnp.int32, x.shape, 1) % (2 * S5_STATE) < S5_STATE
    width = x.shape[1]
    for k in range(n_double):
        sh = jnp.where(row >= 2 ** k, pltpu.roll(x, 2 ** k, 0), 0.0)
        swapped = jnp.where(is_re, pltpu.roll(sh, width - S5_STATE, 1), pltpu.roll(sh, S5_STATE, 1))
        x = x + st_ref[0, 2 * k:2 * k + 1, :] * sh + st_ref[0, 2 * k + 1:2 * k + 2, :] * swapped
    xin = jnp.where(row >= 1, pltpu.roll(x, 1, 0), 0.0).astype(BF16)
    y = jnp.dot(u, tt_sc[...], preferred_element_type=F32) + jnp.dot(xin, cc_sc[...], preferred_element_type=F32)
    for t in range(n):
        y_ref[:, t, :] = y[:, t * w:(t + 1) * w]


def _s5_scan(u, tables, n_batch, seq):
    klag, bend, cin, st = tables
    n_chunk = seq // S5_CHUNK
    n_slab = klag.shape[0]
    w = S5_SLAB * S5_GROUP
    u3 = u.reshape(n_batch * n_chunk, S5_CHUNK, S5_WIDTH)
    per_slab = lambda a: pl.BlockSpec((1,) + a.shape[1:], lambda j, b: (j,) + (0,) * (a.ndim - 1))
    io_spec = pl.BlockSpec((n_chunk, S5_CHUNK, w), lambda j, b: (b, 0, j))
    y = pl.pallas_call(
        functools.partial(_s5_kernel, n_double=st.shape[1] // 2),
        grid=(n_slab, n_batch),
        in_specs=[io_spec, per_slab(klag), per_slab(bend), per_slab(cin), per_slab(st)],
        out_specs=io_spec,
        out_shape=jax.ShapeDtypeStruct(u3.shape, F32),
        scratch_shapes=[pltpu.VMEM((S5_CHUNK * w, S5_CHUNK * w), BF16),
                        pltpu.VMEM((S5_CHUNK * w, S5_SLAB * 2 * S5_STATE), BF16),
                        pltpu.VMEM((S5_SLAB * 2 * S5_STATE, S5_CHUNK * w), BF16)],
        compiler_params=_cparams(2),
        name="s5_chunk",
    )(u3, klag, bend, cin, st)
    return y.reshape(u.shape)


def _attn_kernel(lam_ref, sg_ref, q_ref, k_ref, vt_ref, o_ref, m_sc, acc_sc, *, tq, lambda_init):
    tk = vt_ref.shape[2]
    qi = pl.program_id(2)
    q = q_ref[...]
    lane = lax.broadcasted_iota(jnp.int32, q.shape, 1)
    zero = jnp.zeros_like(q)
    q2 = jnp.concatenate([jnp.where(lane < DA_HEAD_DIM, q, zero), jnp.where(lane >= DA_HEAD_DIM, q, zero)], axis=0)
    m_sc[...] = jnp.full(m_sc.shape, NEG_BIG, F32)
    acc_sc[...] = jnp.zeros(acc_sc.shape, F32)

    n_strip = 2 * tq // ATTN_STRIP
    n_sub = tq // tk
    ones_rows = jnp.ones((16, tk), BF16)

    def step(j, masked):
        kb = k_ref[pl.ds(pl.multiple_of(j * tq, tq), tq), :]
        vtb = [jnp.concatenate([vt_ref[j * n_sub + i], ones_rows], axis=0) for i in range(n_sub)]
        n_key = [(c * ATTN_STRIP) % tq + ATTN_STRIP if masked else tq for c in range(n_strip)]
        s = [lax.dot_general(kb[:n_key[c]], q2[c * ATTN_STRIP:(c + 1) * ATTN_STRIP], (((1,), (1,)), ((), ())),
                             preferred_element_type=F32) for c in range(n_strip)]
        for c in range(n_strip):
            cs = slice(c * ATTN_STRIP, (c + 1) * ATTN_STRIP)
            sc = s[c]
            if masked:
                key = lax.broadcasted_iota(jnp.int32, sc.shape, 0)
                col = lax.broadcasted_iota(jnp.int32, sc.shape, 1)
                sc = jnp.where(key <= col + (c * ATTN_STRIP) % tq, sc, NEG_BIG)
            m_prev = m_sc[:, cs]
            m_next = jnp.maximum(m_prev, jnp.max(sc, axis=0, keepdims=True))
            alpha = jnp.exp2(m_prev - m_next)
            p = jnp.exp2(sc - m_next).astype(BF16)
            m_sc[:, cs] = m_next
            pv = sum(jnp.dot(vtb[i][:, :min(tk, n_key[c] - i * tk)], p[i * tk:min((i + 1) * tk, n_key[c])],
                             preferred_element_type=F32) for i in range(n_sub) if i * tk < n_key[c])
            acc_sc[:, cs] = alpha * acc_sc[:, cs] + pv

    def body(jj, carry):
        step(2 * jj, False)
        step(2 * jj + 1, False)
        return carry

    lax.fori_loop(0, qi // 2, body, 0)

    @pl.when(qi % 2 == 1)
    def _():
        step(qi - 1, False)

    step(qi, True)

    lv = lam_ref[...]
    lam = (jnp.exp(jnp.sum(lv[0:1] * lv[1:2], axis=1, keepdims=True))
           - jnp.exp(jnp.sum(lv[2:3] * lv[3:4], axis=1, keepdims=True)) + lambda_init)
    o_all = acc_sc[:DA_V_DIM, :] / acc_sc[DA_V_DIM:DA_V_DIM + 1, :]
    o = (o_all[:, :tq] - lam * o_all[:, tq:]).T
    o = o * lax.rsqrt(jnp.mean(o * o, axis=-1, keepdims=True) + DA_SUBLN_EPS) * sg_ref[...] * (1.0 - lambda_init)
    o_ref[...] = o.astype(o_ref.dtype)


def _diff_attn(q, k, vt, lam_vecs, subln_g, lambda_init, n_batch, seq):
    tk = vt.shape[2]
    tq = min(ATTN_TQ, seq)
    nq = seq // tq
    q_spec = pl.BlockSpec((tq, DA_V_DIM), lambda b, h, i: (b * nq + i, h))
    k_spec = pl.BlockSpec((seq, DA_V_DIM), lambda b, h, i: (b, h))
    vt_spec = pl.BlockSpec((seq // tk, DA_V_DIM, tk), lambda b, h, i: (b, h, 0))
    return pl.pallas_call(
        functools.partial(_attn_kernel, tq=tq, lambda_init=lambda_init),
        grid=(n_batch, DA_HEADS, nq),
        in_specs=[_const_spec(lam_vecs.shape), _const_spec((1, DA_V_DIM)), q_spec, k_spec, vt_spec],
        out_specs=q_spec,
        out_shape=jax.ShapeDtypeStruct((n_batch * seq, DA_WIDTH), BF16),
        scratch_shapes=[pltpu.VMEM((1, 2 * tq), F32), pltpu.VMEM((DA_V_DIM + 16, 2 * tq), F32)],
        compiler_params=_cparams(3),
        name="diff_attn",
    )(lam_vecs, subln_g[None, :], q, k, vt)


def _rw_prep_kernel(*refs, tm, seq, gated):
    if gated:
        (z_ref, zp_ref, vf_ref, mu_ref, w0_ref, a0_ref, kk_ref, ka_ref, rk_ref, w2h_ref, w2l_ref, a2h_ref, a2l_ref,
         g2h_ref, g2l_ref, e64_ref, tri_ref, v0_ref, v1h_ref, v1l_ref, v2h_ref, v2l_ref,
         rh_ref, ah_ref, bh_ref, kh_ref, bt_ref, kt_ref, v_ref, gc_ref, bonus_ref, g_ref) = refs
    else:
        (z_ref, zp_ref, mu_ref, w0_ref, a0_ref, kk_ref, ka_ref, rk_ref, w2h_ref, w2l_ref, a2h_ref, a2l_ref,
         g2h_ref, g2l_ref, e64_ref, tri_ref,
         rh_ref, ah_ref, bh_ref, kh_ref, bt_ref, kt_ref, v_ref, gc_ref, bonus_ref, g_ref, vfirst_ref) = refs
    i = pl.program_id(0)
    z = z_ref[...]
    first = jnp.where((i * tm) % seq == 0, 0.0, 1.0) * zp_ref[7:8, :]
    row = lax.broadcasted_iota(jnp.int32, z.shape, 0)
    prev = jnp.where(row == 0, first, pltpu.roll(z, 1, 0))
    zf = z + (prev - z) * mu_ref[...]
    r = zf[:, :RW_WIDTH]
    k = zf[:, RW_WIDTH:2 * RW_WIDTH]
    v = zf[:, 2 * RW_WIDTH:3 * RW_WIDTH]
    tail = zf[:, 3 * RW_WIDTH:]
    w_in = w0_ref[...] + _mm_split(jnp.tanh(tail), w2h_ref[...], w2l_ref[...])
    softplus = jnp.maximum(-w_in, 0.0) + jnp.log(1.0 + jnp.exp(-jnp.abs(w_in)))
    logw = -jnp.exp(-softplus - 0.5)
    a = _sigmoid(a0_ref[...] + _mm_split(tail, a2h_ref[...], a2l_ref[...]))
    g_ref[...] = _mm_split(_sigmoid(tail), g2h_ref[...], g2l_ref[...])
    kk = k * kk_ref[...]
    norm = jnp.sqrt(_mm_sel_right(kk * kk, e64_ref[...], GROUP_SUM_TERMS))
    kk = kk / jnp.maximum(norm, 1e-12)
    k = k * (1.0 + (a - 1.0) * ka_ref[...])
    if gated:
        lo = _mm_split(v, v1h_ref[...], v1l_ref[...])
        mix = _sigmoid(v0_ref[...] + _mm_split(lo, v2h_ref[...], v2l_ref[...]))
        v = v + (vf_ref[...] - v) * mix
    else:
        vfirst_ref[...] = v
    bonus_ref[...] = _mm_sel_right(r * k * rk_ref[...], e64_ref[...], GROUP_SUM_TERMS) * v
    cs = _mm_sel_left(tri_ref[...], logw, 2)
    cs_end = jnp.concatenate(
        [jnp.broadcast_to(cs[c * RW_CHUNK + RW_CHUNK - 1:(c + 1) * RW_CHUNK, :], (RW_CHUNK, RW_WIDTH))
         for c in range(tm // RW_CHUNK)], axis=0)
    inv = jnp.exp(-cs)
    to_end = jnp.exp(cs_end - cs)
    b = kk * a
    rh_ref[...] = (r * jnp.exp(cs)).astype(BF16)
    ah_ref[...] = (-kk * jnp.exp(cs - logw)).astype(BF16)
    bh_ref[...] = (b * inv).astype(BF16)
    kh_ref[...] = (k * inv).astype(BF16)
    bt_ref[...] = (b * to_end).astype(BF16)
    kt_ref[...] = (k * to_end).astype(BF16)
    v_ref[...] = v.astype(BF16)
    gc_ref[...] = jnp.exp(cs_end)


def _pad_rows(w, start):
    return jnp.zeros((RW_TAIL, w.shape[1]), w.dtype).at[start:start + w.shape[0]].set(w)


def _hi_lo(w):
    hi = w.astype(BF16)
    return hi, (w - hi.astype(w.dtype)).astype(BF16)


def _rw_prep(z, p, v_first, seq, tm):
    t = z.shape[0]
    gated = v_first is not None
    idx = jnp.arange(tm)
    same_chunk = (idx[:, None] // RW_CHUNK) == (idx[None, :] // RW_CHUNK)
    tri = (same_chunk & (idx[None, :] <= idx[:, None])).astype(BF16)
    vec = lambda a: a[None, :]
    consts = [vec(p['mu']), vec(p['w0']), vec(p['a0']), vec(p['k_k']), vec(p['k_a']), vec(p['r_k'].reshape(-1)),
              *_hi_lo(_pad_rows(p['w2'], 0)), *_hi_lo(_pad_rows(p['a2'], RW_DECAY_LORA)),
              *_hi_lo(_pad_rows(p['g2'], RW_DECAY_LORA + RW_A_LORA)),
              _group_ones(RW_WIDTH, RW_HEAD), tri]
    if gated:
        consts += [vec(p['v0']), *_hi_lo(p['v1']), *_hi_lo(p['v2'])]
    row = lambda width: pl.BlockSpec((tm, width), lambda i: (i, 0))
    zp_spec = pl.BlockSpec((8, RW_SHIFT_WIDTH), lambda i: (jnp.maximum(i * (tm // 8) - 1, 0), 0))
    in_specs = [row(RW_SHIFT_WIDTH), zp_spec] + ([row(RW_WIDTH)] if gated else []) + [_const_spec(c.shape) for c in consts]
    args = [z, z] + ([v_first] if gated else []) + consts
    n_out = 10 if gated else 11
    dtypes = [BF16] * 7 + [F32] * 3 + ([] if gated else [F32])
    outs = pl.pallas_call(
        functools.partial(_rw_prep_kernel, tm=tm, seq=seq, gated=gated),
        grid=(t // tm,),
        in_specs=in_specs,
        out_specs=[row(RW_WIDTH)] * n_out,
        out_shape=[jax.ShapeDtypeStruct((t, RW_WIDTH), dt) for dt in dtypes],
        compiler_params=_cparams(1),
        name="rw_prep",
    )(*args)
    return outs


def _rw_intra_kernel(rh_ref, ah_ref, bh_ref, kh_ref, bt_ref, kt_ref, v_ref, gc_ref,
                     wr_ref, mb_ref, uloc_ref, oloc_ref, kvloc_ref, gcol_ref, *, n_sub):
    n = RW_CHUNK
    ri = lax.broadcasted_iota(jnp.int32, (n, n), 0)
    ci = lax.broadcasted_iota(jnp.int32, (n, n), 1)
    strict = ci < ri
    incl = ci <= ri
    eye = ci == ri
    eye_b = eye.astype(BF16)
    ones_b = jnp.ones((n, n), BF16)
    items = [(c, h) for c in range(n_sub) for h in range(RW_HEADS)]
    rows = lambda c: slice(c * n, (c + 1) * n)
    lanes = lambda h: slice(h * RW_HEAD, (h + 1) * RW_HEAD)
    ld = lambda ref: [ref[rows(c), lanes(h)] for c, h in items]
    rh, ah, bh, kh, bt, kt, v = (ld(r) for r in (rh_ref, ah_ref, bh_ref, kh_ref, bt_ref, kt_ref, v_ref))
    ar = [jnp.concatenate([a, r], axis=0) for a, r in zip(ah, rh)]
    gb = [_mm_nt(x, y) for x, y in zip(ar, bh)]
    gk = [_mm_nt(x, y) for x, y in zip(ar, kh)]
    l_ab = [jnp.where(strict, g[:n], 0.0) for g in gb]
    m_rb = [jnp.where(incl, g[n:], 0.0).astype(BF16) for g in gb]
    lm_k = [jnp.concatenate([jnp.where(strict, g[:n], 0.0), jnp.where(incl, g[n:], 0.0)], axis=0).astype(BF16) for g in gk]
    akv = [_mm(x, y) for x, y in zip(lm_k, v)]
    tinv = [jnp.where(eye, 1.0, x) for x in l_ab]
    l_ab = [x.astype(BF16) for x in l_ab]
    pw = [_mm(x, x) for x in l_ab]
    for _ in range(4):
        pw_b = [x.astype(BF16) for x in pw]
        res = [_mm(jnp.concatenate([p, t.astype(BF16)], axis=0), p) for p, t in zip(pw_b, tinv)]
        pw = [r[:n] for r in res]
        tinv = [t + r[n:] for t, r in zip(tinv, res)]
    tinv = [(t + _mm(t, p)).astype(BF16) for t, p in zip(tinv, pw)]
    w = [_mm(t, a) for t, a in zip(tinv, ah)]
    uloc = [_mm(t, x[:n]) for t, x in zip(tinv, akv)]
    kvloc = [_mm_tn(x, y) for x, y in zip(kt, v)]
    bt_t = [_mm_tn(x, eye_b) for x in bt]
    for i, (c, h) in enumerate(items):
        decay = jnp.where(eye, jnp.broadcast_to(gc_ref[(c + 1) * n - 1:(c + 1) * n, lanes(h)], (n, n)), 0.0)
        gcol_ref[rows(c), lanes(h)] = _mm_sel_right(decay, ones_b, 3)
        wr_ref[2 * c * n:(2 * c + 1) * n, lanes(h)] = w[i].astype(BF16)
        wr_ref[(2 * c + 1) * n:(2 * c + 2) * n, lanes(h)] = rh[i]
        mb_ref[2 * c * n:(2 * c + 1) * n, lanes(h)] = m_rb[i]
        mb_ref[(2 * c + 1) * n:(2 * c + 2) * n, lanes(h)] = bt_t[i].astype(BF16)
        uloc_ref[rows(c), lanes(h)] = uloc[i]
        oloc_ref[rows(c), lanes(h)] = akv[i][n:]
        kvloc_ref[rows(c), lanes(h)] = kvloc[i]


def _rw_seq_kernel(wr_ref, mb_ref, uloc_ref, oloc_ref, kvloc_ref, gcol_ref, y_ref, h_sc, *, n_batch):
    @pl.when(pl.program_id(0) == 0)
    def _():
        h_sc[...] = jnp.zeros(h_sc.shape, F32)

    n = RW_CHUNK
    blk = (lax.broadcasted_iota(jnp.int32, (RW_WIDTH, RW_WIDTH), 0) // RW_HEAD
           == lax.broadcasted_iota(jnp.int32, (RW_WIDTH, RW_WIDTH), 1) // RW_HEAD)
    block_diag = lambda x: jnp.where(blk, jnp.concatenate([x] * RW_HEADS, axis=0), 0.0).astype(BF16)
    batches = range(n_batch)
    h0 = [h_sc[b] for b in batches]
    s1 = [jnp.dot(wr_ref[b], block_diag(h0[b]), preferred_element_type=F32) for b in batches]
    u = [s1[b][:n] + uloc_ref[b] for b in batches]
    s2 = [jnp.dot(mb_ref[b], block_diag(u[b]), preferred_element_type=F32) for b in batches]
    for b in batches:
        y_ref[b] = s1[b][n:] + s2[b][:n] + oloc_ref[b]
        h_sc[b] = gcol_ref[b] * h0[b] + s2[b][n:] + kvloc_ref[b]


def _rw_scan(ops, n_batch, seq):
    t = n_batch * seq
    n_sub = 4
    rows = n_sub * RW_CHUNK
    spec = pl.BlockSpec((rows, RW_WIDTH), lambda i: (i, 0))
    spec2 = pl.BlockSpec((2 * rows, RW_WIDTH), lambda i: (i, 0))
    wr, mb, uloc, oloc, kvloc, gcol = pl.pallas_call(
        functools.partial(_rw_intra_kernel, n_sub=n_sub),
        grid=(t // rows,),
        in_specs=[spec] * 8,
        out_specs=[spec2, spec2, spec, spec, spec, spec],
        out_shape=[jax.ShapeDtypeStruct((2 * t, RW_WIDTH), BF16)] * 2 + [jax.ShapeDtypeStruct((t, RW_WIDTH), F32)] * 4,
        compiler_params=_cparams(1),
        name="rw_intra",
    )(*ops)
    n_chunk = seq // RW_CHUNK
    b3 = lambda a: a.reshape(n_batch, -1, RW_WIDTH)
    cspec = pl.BlockSpec((n_batch, RW_CHUNK, RW_WIDTH), lambda c: (0, c, 0))
    cspec2 = pl.BlockSpec((n_batch, 2 * RW_CHUNK, RW_WIDTH), lambda c: (0, c, 0))
    y = pl.pallas_call(
        functools.partial(_rw_seq_kernel, n_batch=n_batch),
        grid=(n_chunk,),
        in_specs=[cspec2, cspec2, cspec, cspec, cspec, cspec],
        out_specs=cspec,
        out_shape=jax.ShapeDtypeStruct((n_batch, seq, RW_WIDTH), F32),
        scratch_shapes=[pltpu.VMEM((n_batch, RW_HEAD, RW_WIDTH), F32)],
        compiler_params=_cparams(1),
        name="rw_seq",
    )(b3(wr), b3(mb), b3(uloc), b3(oloc), b3(kvloc), b3(gcol))
    return y.reshape(t, RW_WIDTH)


def _merge_kernel(x_ref, ys5_ref, us5_ref, yb_ref, yrw_ref, bonus_ref, g_ref, gl_ref, d_ref, wglu_ref, lng_ref, lnb_ref,
                  e64_ref, wb_ref, wout_ref, o_ref):
    ya = ys5_ref[...] + d_ref[...] * us5_ref[...]
    ya = 0.5 * ya * (1.0 + jnp.tanh(math.sqrt(2.0 / math.pi) * (ya + 0.044715 * (ya * ya * ya))))
    ya = ya * _sigmoid(_mm(ya, wglu_ref[...]))
    y = yrw_ref[...]
    mean = _mm_sel_right(y, e64_ref[...], GROUP_SUM_TERMS) * (1.0 / RW_HEAD)
    yc = y - mean
    var = _mm_sel_right(yc * yc, e64_ref[...], GROUP_SUM_TERMS) * (1.0 / RW_HEAD)
    yc = yc * lax.rsqrt(var + RW_LN_EPS) * lng_ref[...] + lnb_ref[...]
    yc = (yc + bonus_ref[...]) * g_ref[...]
    merged = jnp.zeros((x_ref.shape[0], D_MODEL), F32)
    for n, br in enumerate((ya, yb_ref[...], yc)):
        gate = _sigmoid(gl_ref[:, n * D_MODEL:(n + 1) * D_MODEL])
        merged = merged + gate * _mm(br, wb_ref[n])
    o_ref[...] = x_ref[...] + _mm(merged, wout_ref[...])


def _merge(x2, ys5, us5, yb, yrw, bonus, g, gl, s5_d, w_glu, ln_g, ln_b, w_branch, w_out, tm):
    t = x2.shape[0]
    consts = [s5_d[None, :], w_glu.astype(BF16), ln_g[None, :], ln_b[None, :], _group_ones(RW_WIDTH, RW_HEAD),
              w_branch.astype(BF16), w_out.astype(BF16)]
    row = lambda width: pl.BlockSpec((tm, width), lambda i: (i, 0))
    widths = [D_MODEL, S5_WIDTH, S5_WIDTH, DA_WIDTH, RW_WIDTH, RW_WIDTH, RW_WIDTH, N_BRANCH * D_MODEL]
    return pl.pallas_call(
        _merge_kernel,
        grid=(t // tm,),
        in_specs=[row(wd) for wd in widths] + [_const_spec(c.shape) for c in consts],
        out_specs=row(D_MODEL),
        out_shape=jax.ShapeDtypeStruct((t, D_MODEL), F32),
        compiler_params=_cparams(1),
        name="merge",
    )(x2, ys5, us5, yb, yrw, bonus, g, gl, *consts)


def _ffn_kernel(x_ref, g_ref, win_ref, wout_ref, o_ref):
    x = x_ref[...]
    h = (x * lax.rsqrt(jnp.mean(x * x, axis=-1, keepdims=True) + NORM_EPS) * g_ref[...]).astype(BF16)
    gate = jnp.dot(h, win_ref[:, :D_FF], preferred_element_type=F32)
    up = jnp.dot(h, win_ref[:, D_FF:], preferred_element_type=F32)
    act = gate * _sigmoid(gate) * up
    o_ref[...] = x + _mm(act, wout_ref[...])


def _ffn(x2, g, w_in, w_out, tm):
    t = x2.shape[0]
    consts = [g[None, :], w_in.astype(BF16), w_out.astype(BF16)]
    row = pl.BlockSpec((tm, D_MODEL), lambda i: (i, 0))
    return pl.pallas_call(
        _ffn_kernel,
        grid=(t // tm,),
        in_specs=[row] + [_const_spec(c.shape) for c in consts],
        out_specs=row,
        out_shape=jax.ShapeDtypeStruct((t, D_MODEL), F32),
        compiler_params=_cparams(1),
        name="ffn",
    )(x2, *consts)


def _layer(x2, i, n_batch, seq, p, v_first, tm):
    us5, q, k, v, zrw, gl = _in_proj(x2, p['norm1_g'], p['w_in'], p['da_q_gain'], p['da_k_gain'], tm)
    n_double = max(1, (seq // S5_CHUNK - 1).bit_length())
    tables = _s5_tables(p['s5_lambda_re'], p['s5_lambda_im'], p['s5_log_dt'], p['s5_b_re'], p['s5_b_im'],
                        p['s5_c_re'], p['s5_c_im'], n_double)
    ys5 = _s5_scan(us5, tables, n_batch, seq)
    lambda_init = 0.8 - 0.6 * math.exp(-0.3 * i)
    yb = _diff_attn(q, k, v, p['da_lambda'], p['da_subln_g'], lambda_init, n_batch, seq)
    rw = {name[3:]: val for name, val in p.items() if name.startswith('rw_')}
    outs = _rw_prep(zrw, rw, v_first, seq, min(RW_PREP_TILE, seq))
    if v_first is None:
        v_first = outs[10]
    yrw = _rw_scan(outs[:8], n_batch, seq)
    x2 = _merge(x2, ys5, us5, yb, yrw, outs[8], outs[9], gl, p['s5_d'], p['s5_w_glu'], p['rw_ln_g'], p['rw_ln_b'],
                p['w_branch'], p['w_out'], tm)
    x2 = _ffn(x2, p['norm2_g'], p['w_ffn_in'], p['w_ffn_out'], tm)
    return x2, v_first


def kernel(x, norm1_g, w_in, s5_lambda_re, s5_lambda_im, s5_log_dt, s5_b_re, s5_b_im, s5_c_re, s5_c_im, s5_d, s5_w_glu, da_q_gain, da_k_gain, da_lambda, da_subln_g, rw_mu, rw_w0, rw_w2, rw_a0, rw_a2, rw_g2, rw_k_k, rw_k_a, rw_r_k, rw_ln_g, rw_ln_b, rw_v0, rw_v1, rw_v2, w_branch, w_out, norm2_g, w_ffn_in, w_ffn_out):
    n_batch, seq, _ = x.shape
    per_layer = dict(norm1_g=norm1_g, w_in=w_in, s5_lambda_re=s5_lambda_re, s5_lambda_im=s5_lambda_im,
                     s5_log_dt=s5_log_dt, s5_b_re=s5_b_re, s5_b_im=s5_b_im, s5_c_re=s5_c_re, s5_c_im=s5_c_im,
                     s5_d=s5_d, s5_w_glu=s5_w_glu, da_q_gain=da_q_gain, da_k_gain=da_k_gain, da_lambda=da_lambda,
                     da_subln_g=da_subln_g, rw_mu=rw_mu, rw_w0=rw_w0, rw_w2=rw_w2, rw_a0=rw_a0, rw_a2=rw_a2,
                     rw_g2=rw_g2, rw_k_k=rw_k_k, rw_k_a=rw_k_a, rw_r_k=rw_r_k, rw_ln_g=rw_ln_g, rw_ln_b=rw_ln_b,
                     w_branch=w_branch, w_out=w_out, norm2_g=norm2_g, w_ffn_in=w_ffn_in, w_ffn_out=w_ffn_out)
    tm = min(TOKEN_TILE, seq)
    x2 = x.reshape(n_batch * seq, D_MODEL)
    v_first = None
    for i in range(w_in.shape[0]):
        p = {name: val[i] for name, val in per_layer.items()}
        if i > 0:
            p.update(rw_v0=rw_v0[i - 1], rw_v1=rw_v1[i - 1], rw_v2=rw_v2[i - 1])
        x2, v_first = _layer(x2, i, n_batch, seq, p, v_first, tm)
    return x2.reshape(x.shape)
```

```python
import functools
import math

import jax
import jax.numpy as jnp
from jax import lax
from jax.experimental import pallas as pl
from jax.experimental.pallas import tpu as pltpu

F32 = jnp.float32
BF16 = jnp.bfloat16

D_MODEL = 1024
DEPTH = 2
S5_WIDTH = 512
S5_GROUP = 16
S5_GROUPS = S5_WIDTH // S5_GROUP
S5_STATE = 64
S5_CHUNK = 16
S5_SLAB = 8
DA_HEADS = 4
DA_HEAD_DIM = 64
DA_V_DIM = 2 * DA_HEAD_DIM
DA_WIDTH = DA_HEADS * DA_V_DIM
DA_SUBLN_EPS = 1e-5
RW_HEAD = 64
RW_WIDTH = 512
RW_HEADS = RW_WIDTH // RW_HEAD
RW_DECAY_LORA = 32
RW_A_LORA = 32
RW_G_LORA = 96
RW_TAIL = RW_DECAY_LORA + RW_A_LORA + RW_G_LORA
RW_SHIFT_WIDTH = 3 * RW_WIDTH + RW_TAIL
RW_LN_EPS = 64e-5
RW_CHUNK = 64
N_BRANCH = 3
D_FF = 2816
NORM_EPS = 1e-6
NEG_BIG = -1e30
ATTN_TQ = 512
ATTN_STRIP = 256
TOKEN_TILE = 512
GROUP_SUM_TERMS = 1
RW_PREP_TILE = 256

VMEM_LIMIT = 56 * 1024 * 1024


def _cparams(n_axes):
    return pltpu.CompilerParams(dimension_semantics=("arbitrary",) * n_axes, vmem_limit_bytes=VMEM_LIMIT)


def _const_spec(shape):
    nd = len(shape)
    return pl.BlockSpec(shape, lambda *_: (0,) * nd, pipeline_mode=pl.Buffered(1))


def _mm(a, b):
    return jnp.dot(a.astype(BF16), b.astype(BF16), preferred_element_type=F32)


def _mm_nt(a, b):
    return lax.dot_general(a.astype(BF16), b.astype(BF16), (((1,), (1,)), ((), ())), preferred_element_type=F32)


def _mm_tn(a, b):
    return lax.dot_general(a.astype(BF16), b.astype(BF16), (((0,), (0,)), ((), ())), preferred_element_type=F32)


def _split2(x):
    hi = x.astype(BF16)
    lo = (x - hi.astype(F32)).astype(BF16)
    return hi, lo


def _split3(x):
    hi = x.astype(BF16)
    r = x - hi.astype(F32)
    mid = r.astype(BF16)
    lo = (r - mid.astype(F32)).astype(BF16)
    return hi, mid, lo


def _bf16_terms(x, terms):
    return {1: lambda: (x.astype(BF16),), 2: lambda: _split2(x), 3: lambda: _split3(x)}[terms]()


def _mm_sel_right(x, sel, terms):
    return sum(jnp.dot(p, sel, preferred_element_type=F32) for p in _bf16_terms(x, terms))


def _mm_sel_left(sel, x, terms):
    return sum(jnp.dot(sel, p, preferred_element_type=F32) for p in _bf16_terms(x, terms))


def _mm_split(x, w_hi, w_lo):
    x_hi, x_lo = _split2(x)
    d = lambda p, q: jnp.dot(p, q, preferred_element_type=F32)
    return d(x_hi, w_hi) + d(x_hi, w_lo) + d(x_lo, w_hi)


def _sigmoid(x):
    return 1.0 / (1.0 + jnp.exp(-x))


def _group_ones(width, group):
    r = jnp.arange(width) // group
    return (r[:, None] == r[None, :]).astype(BF16)


def _in_proj_kernel(x_ref, g_ref, ws5_ref, wq_ref, wk_ref, wvt_ref, wrw_ref, wgate_ref, e64_ref, qg_ref, kg_ref,
                    us5_ref, q_ref, k_ref, vt_ref, zrw_ref, gl_ref):
    x = x_ref[...]
    h = x * lax.rsqrt(jnp.mean(x * x, axis=-1, keepdims=True) + NORM_EPS) * g_ref[...]
    h = h.astype(BF16)
    dot = lambda w_ref: jnp.dot(h, w_ref[...], preferred_element_type=F32)
    us5_ref[...] = dot(ws5_ref)

    def qk_norm(t, gain_ref):
        ms = _mm_sel_right(t * t, e64_ref[...], GROUP_SUM_TERMS) * (1.0 / DA_HEAD_DIM)
        return t * lax.rsqrt(ms + NORM_EPS) * gain_ref[...]

    q_ref[...] = qk_norm(dot(wq_ref), qg_ref).astype(BF16)
    k_ref[...] = qk_norm(dot(wk_ref), kg_ref).astype(BF16)
    vt_ref[0] = lax.dot_general(wvt_ref[...], h, (((1,), (1,)), ((), ())), preferred_element_type=F32).astype(BF16)
    zrw_ref[...] = dot(wrw_ref)
    gl_ref[...] = dot(wgate_ref)


def _in_proj(x2, g, w, q_gain, k_gain, tm):
    t = x2.shape[0]
    w = w.astype(BF16)
    o = 0
    parts = []
    for width in (S5_WIDTH, DA_WIDTH, DA_WIDTH, DA_WIDTH, RW_SHIFT_WIDTH, N_BRANCH * D_MODEL):
        parts.append(w[:, o:o + width])
        o += width
    parts[3] = parts[3].T
    e64 = _group_ones(DA_WIDTH, DA_HEAD_DIM)
    qg = jnp.tile(q_gain.reshape(-1), DA_HEADS)[None, :] * (DA_HEAD_DIM ** -0.5 * math.log2(math.e))
    kg = jnp.tile(k_gain.reshape(-1), DA_HEADS)[None, :]
    row = lambda width: pl.BlockSpec((tm, width), lambda i: (i, 0))
    consts = [g[None, :]] + parts + [e64, qg, kg]
    out_specs = [row(S5_WIDTH), row(DA_WIDTH), row(DA_WIDTH), pl.BlockSpec((1, DA_WIDTH, tm), lambda i: (i, 0, 0)),
                 row(RW_SHIFT_WIDTH), row(N_BRANCH * D_MODEL)]
    out_shape = [jax.ShapeDtypeStruct((t, S5_WIDTH), F32), jax.ShapeDtypeStruct((t, DA_WIDTH), BF16),
                 jax.ShapeDtypeStruct((t, DA_WIDTH), BF16), jax.ShapeDtypeStruct((t // tm, DA_WIDTH, tm), BF16),
                 jax.ShapeDtypeStruct((t, RW_SHIFT_WIDTH), F32), jax.ShapeDtypeStruct((t, N_BRANCH * D_MODEL), F32)]
    return pl.pallas_call(
        _in_proj_kernel,
        grid=(t // tm,),
        in_specs=[row(D_MODEL)] + [_const_spec(c.shape) for c in consts],
        out_specs=out_specs,
        out_shape=out_shape,
        compiler_params=_cparams(1),
        name="in_proj",
    )(x2, *consts)


def _s5_tables(lam_re, lam_im, log_dt, b_re, b_im, c_re, c_im, n_double):
    n = S5_CHUNK
    dt = jnp.exp(log_dt)[:, None]
    den = lam_re * lam_re + lam_im * lam_im

    def a_pow(tau):
        mag = jnp.exp(lam_re * dt * tau)
        return mag * jnp.cos(lam_im * dt * tau), mag * jnp.sin(lam_im * dt * tau)

    ar, ai = a_pow(1.0)
    nr = ar - 1.0
    kr = (nr * lam_re + ai * lam_im) / den
    ki = (ai * lam_re - nr * lam_im) / den
    bbar_r = kr[..., None] * b_re - ki[..., None] * b_im
    bbar_i = kr[..., None] * b_im + ki[..., None] * b_re
    taus = jnp.arange(n + 1, dtype=lam_re.dtype)
    pr, pi = jax.vmap(a_pow)(taus)
    hp = lax.Precision.HIGHEST
    abr = pr[..., None] * bbar_r[None] - pi[..., None] * bbar_i[None]
    abi = pr[..., None] * bbar_i[None] + pi[..., None] * bbar_r[None]
    ktau = (jnp.einsum('gdp,ngpc->ngdc', c_re, abr, precision=hp)
            - jnp.einsum('gdp,ngpc->ngdc', c_im, abi, precision=hp))
    g, q, c, p2 = lam_re.shape[0], S5_SLAB, S5_GROUP, 2 * S5_STATE
    j = g // q
    eye = jnp.eye(q, dtype=ktau.dtype)
    kl = ktau[:n].transpose(1, 0, 3, 2).reshape(j, q, n, c, c).transpose(0, 2, 1, 3, 4)
    klag = (kl[:, :, :, :, None, :] * eye[None, None, :, None, :, None]).reshape(j, n, q * c, q * c)
    rev = n - 1 - jnp.arange(n)
    b_end = jnp.concatenate([abr[rev], abi[rev]], axis=2)
    bend = b_end.reshape(n, j, q, p2, c).transpose(1, 0, 2, 4, 3).reshape(j, n, q * c, p2)
    cr = c_re[None] * pr[1:, :, None, :] - c_im[None] * pi[1:, :, None, :]
    ci = c_re[None] * pi[1:, :, None, :] + c_im[None] * pr[1:, :, None, :]
    c_in = jnp.concatenate([cr, -ci], axis=3)
    cin = c_in.reshape(n, j, q, c, p2).transpose(1, 0, 4, 2, 3).reshape(j, n, p2, q * c)
    rows = []
    for k in range(n_double):
        sr, si = a_pow(float(n * 2 ** k))
        rows += [jnp.concatenate([sr, sr], axis=1), jnp.concatenate([-si, si], axis=1)]
    step = jnp.stack(rows, axis=1)
    step = step.reshape(j, q, 2 * n_double, p2).transpose(0, 2, 1, 3).reshape(j, 2 * n_double, q * p2)
    return klag.astype(BF16), bend.astype(BF16), cin.astype(BF16), step


def _s5_kernel(u_ref, klag_ref, bend_ref, cin_ref, st_ref, y_ref, tt_sc, bb_sc, cc_sc, *, n_double):
    n, w = S5_CHUNK, S5_SLAB * S5_GROUP

    @pl.when(pl.program_id(1) == 0)
    def _():
        zero = jnp.zeros((w, w), BF16)
        row_q = lax.broadcasted_iota(jnp.int32, (w, w), 0) // S5_GROUP
        lane_q = lax.broadcasted_iota(jnp.int32, (w, w), 1) // S5_GROUP
        for s in range(n):
            for t in range(n):
                tt_sc[s * w:(s + 1) * w, t * w:(t + 1) * w] = klag_ref[0, t - s] if t >= s else zero
            b_blk, c_blk = bend_ref[0, s], cin_ref[0, s]
            for q in range(S5_SLAB):
                bb_sc[s * w:(s + 1) * w, q * w:(q + 1) * w] = jnp.where(row_q == q, b_blk, zero)
                cc_sc[q * w:(q + 1) * w, s * w:(s + 1) * w] = jnp.where(lane_q == q, c_blk, zero)

    u = jnp.concatenate([u_ref[:, s, :].astype(BF16) for s in range(n)], axis=1)
    x = jnp.dot(u, bb_sc[...], preferred_element_type=F32)
    row = lax.broadcasted_iota(jnp.int32, x.shape, 0)
    n_chunk, width = x.shape

    def swap(a):
        is_re = lax.broadcasted_iota(jnp.int32, a.shape, 1) % (2 * S5_STATE) < S5_STATE
        return jnp.where(is_re, pltpu.roll(a, width - S5_STATE, 1), pltpu.roll(a, S5_STATE, 1))

    for k in range(n_double):
        s, c1, c2 = 2 ** k, st_ref[0, 2 * k:2 * k + 1, :], st_ref[0, 2 * k + 1:2 * k + 2, :]
        if s % 8 == 0 and s < n_chunk:
            lo = x[:n_chunk - s]
            x = jnp.concatenate([x[:s], x[s:] + c1 * lo + c2 * swap(lo)], axis=0)
        else:
            sh = jnp.where(row >= s, pltpu.roll(x, s, 0), 0.0)
            x = x + c1 * sh + c2 * swap(sh)
    xin = jnp.where(row >= 1, pltpu.roll(x, 1, 0), 0.0).astype(BF16)
    y_in = jnp.dot(xin, cc_sc[...], preferred_element_type=F32)
    for t0 in range(0, n, 2):
        rows_used = (t0 + 2) * w
        y_loc = jnp.dot(u[:, :rows_used], tt_sc[:rows_used, t0 * w:(t0 + 2) * w], preferred_element_type=F32)
        for t in (t0, t0 + 1):
            y_ref[:, t, :] = y_loc[:, (t - t0) * w:(t - t0 + 1) * w] + y_in[:, t * w:(t + 1) * w]


def _s5_scan(u, tables, n_batch, seq):
    klag, bend, cin, st = tables
    n_chunk = seq // S5_CHUNK
    n_slab = klag.shape[0]
    w = S5_SLAB * S5_GROUP
    u3 = u.reshape(n_batch * n_chunk, S5_CHUNK, S5_WIDTH)
    per_slab = lambda a: pl.BlockSpec((1,) + a.shape[1:], lambda j, b: (j,) + (0,) * (a.ndim - 1))
    io_spec = pl.BlockSpec((n_chunk, S5_CHUNK, w), lambda j, b: (b, 0, j))
    y = pl.pallas_call(
        functools.partial(_s5_kernel, n_double=st.shape[1] // 2),
        grid=(n_slab, n_batch),
        in_specs=[io_spec, per_slab(klag), per_slab(bend), per_slab(cin), per_slab(st)],
        out_specs=io_spec,
        out_shape=jax.ShapeDtypeStruct(u3.shape, F32),
        scratch_shapes=[pltpu.VMEM((S5_CHUNK * w, S5_CHUNK * w), BF16),
                        pltpu.VMEM((S5_CHUNK * w, S5_SLAB * 2 * S5_STATE), BF16),
                        pltpu.VMEM((S5_SLAB * 2 * S5_STATE, S5_CHUNK * w), BF16)],
        compiler_params=_cparams(2),
        name="s5_chunk",
    )(u3, klag, bend, cin, st)
    return y.reshape(u.shape)


def _attn_kernel(lam_ref, sg_ref, q_ref, k_ref, vt_ref, o_ref, m_sc, acc_sc, *, tq, lambda_init):
    tk = vt_ref.shape[2]
    qi = pl.program_id(2)
    q = q_ref[...]
    lane = lax.broadcasted_iota(jnp.int32, q.shape, 1)
    zero = jnp.zeros_like(q)
    q2 = jnp.concatenate([jnp.where(lane < DA_HEAD_DIM, q, zero), jnp.where(lane >= DA_HEAD_DIM, q, zero)], axis=0)
    m_sc[...] = jnp.full(m_sc.shape, NEG_BIG, F32)
    acc_sc[...] = jnp.zeros(acc_sc.shape, F32)

    n_strip = 2 * tq // ATTN_STRIP
    n_sub = tq // tk
    ones_rows = jnp.ones((16, tk), BF16)

    def step(j, masked):
        kb = k_ref[pl.ds(pl.multiple_of(j * tq, tq), tq), :]
        vtb = [jnp.concatenate([vt_ref[j * n_sub + i], ones_rows], axis=0) for i in range(n_sub)]
        n_key = [(c * ATTN_STRIP) % tq + ATTN_STRIP if masked else tq for c in range(n_strip)]
        s = [lax.dot_general(kb[:n_key[c]], q2[c * ATTN_STRIP:(c + 1) * ATTN_STRIP], (((1,), (1,)), ((), ())),
                             preferred_element_type=F32) for c in range(n_strip)]
        for c in range(n_strip):
            cs = slice(c * ATTN_STRIP, (c + 1) * ATTN_STRIP)
            sc = s[c]
            if masked:
                key = lax.broadcasted_iota(jnp.int32, sc.shape, 0)
                col = lax.broadcasted_iota(jnp.int32, sc.shape, 1)
                sc = jnp.where(key <= col + (c * ATTN_STRIP) % tq, sc, NEG_BIG)
            m_prev = m_sc[:, cs]
            m_next = jnp.maximum(m_prev, jnp.max(sc, axis=0, keepdims=True))
            alpha = jnp.exp2(m_prev - m_next)
            p = jnp.exp2(sc - m_next).astype(BF16)
            m_sc[:, cs] = m_next
            pv = sum(jnp.dot(vtb[i][:, :min(tk, n_key[c] - i * tk)], p[i * tk:min((i + 1) * tk, n_key[c])],
                             preferred_element_type=F32) for i in range(n_sub) if i * tk < n_key[c])
            acc_sc[:, cs] = alpha * acc_sc[:, cs] + pv

    def body(jj, carry):
        step(2 * jj, False)
        step(2 * jj + 1, False)
        return carry

    lax.fori_loop(0, qi // 2, body, 0)

    @pl.when(qi % 2 == 1)
    def _():
        step(qi - 1, False)

    step(qi, True)

    lv = lam_ref[...]
    lam = (jnp.exp(jnp.sum(lv[0:1] * lv[1:2], axis=1, keepdims=True))
           - jnp.exp(jnp.sum(lv[2:3] * lv[3:4], axis=1, keepdims=True)) + lambda_init)
    o_all = acc_sc[:DA_V_DIM, :] / acc_sc[DA_V_DIM:DA_V_DIM + 1, :]
    o = (o_all[:, :tq] - lam * o_all[:, tq:]).T
    o = o * lax.rsqrt(jnp.mean(o * o, axis=-1, keepdims=True) + DA_SUBLN_EPS) * sg_ref[...] * (1.0 - lambda_init)
    o_ref[...] = o.astype(o_ref.dtype)


def _diff_attn(q, k, vt, lam_vecs, subln_g, lambda_init, n_batch, seq):
    tk = vt.shape[2]
    tq = min(ATTN_TQ, seq)
    nq = seq // tq
    q_spec = pl.BlockSpec((tq, DA_V_DIM), lambda b, h, i: (b * nq + i, h))
    k_spec = pl.BlockSpec((seq, DA_V_DIM), lambda b, h, i: (b, h))
    vt_spec = pl.BlockSpec((seq // tk, DA_V_DIM, tk), lambda b, h, i: (b, h, 0))
    return pl.pallas_call(
        functools.partial(_attn_kernel, tq=tq, lambda_init=lambda_init),
        grid=(n_batch, DA_HEADS, nq),
        in_specs=[_const_spec(lam_vecs.shape), _const_spec((1, DA_V_DIM)), q_spec, k_spec, vt_spec],
        out_specs=q_spec,
        out_shape=jax.ShapeDtypeStruct((n_batch * seq, DA_WIDTH), BF16),
        scratch_shapes=[pltpu.VMEM((1, 2 * tq), F32), pltpu.VMEM((DA_V_DIM + 16, 2 * tq), F32)],
        compiler_params=_cparams(3),
        name="diff_attn",
    )(lam_vecs, subln_g[None, :], q, k, vt)


def _rw_prep_kernel(*refs, tm, seq, gated):
    if gated:
        (z_ref, zp_ref, vf_ref, mu_ref, w0_ref, a0_ref, kk_ref, ka_ref, rk_ref, w2h_ref, w2l_ref, a2h_ref, a2l_ref,
         g2h_ref, g2l_ref, e64_ref, tri_ref, v0_ref, v1h_ref, v1l_ref, v2h_ref, v2l_ref,
         rh_ref, ah_ref, bh_ref, kh_ref, bt_ref, kt_ref, v_ref, gc_ref, bonus_ref, g_ref) = refs
    else:
        (z_ref, zp_ref, mu_ref, w0_ref, a0_ref, kk_ref, ka_ref, rk_ref, w2h_ref, w2l_ref, a2h_ref, a2l_ref,
         g2h_ref, g2l_ref, e64_ref, tri_ref,
         rh_ref, ah_ref, bh_ref, kh_ref, bt_ref, kt_ref, v_ref, gc_ref, bonus_ref, g_ref, vfirst_ref) = refs
    i = pl.program_id(0)
    z = z_ref[...]
    first = jnp.where((i * tm) % seq == 0, 0.0, 1.0) * zp_ref[7:8, :]
    row = lax.broadcasted_iota(jnp.int32, z.shape, 0)
    prev = jnp.where(row == 0, first, pltpu.roll(z, 1, 0))
    zf = z + (prev - z) * mu_ref[...]
    r = zf[:, :RW_WIDTH]
    k = zf[:, RW_WIDTH:2 * RW_WIDTH]
    v = zf[:, 2 * RW_WIDTH:3 * RW_WIDTH]
    tail = zf[:, 3 * RW_WIDTH:]
    w_in = w0_ref[...] + _mm_split(jnp.tanh(tail), w2h_ref[...], w2l_ref[...])
    softplus = jnp.maximum(-w_in, 0.0) + jnp.log(1.0 + jnp.exp(-jnp.abs(w_in)))
    logw = -jnp.exp(-softplus - 0.5)
    a = _sigmoid(a0_ref[...] + _mm_split(tail, a2h_ref[...], a2l_ref[...]))
    g_ref[...] = _mm_split(_sigmoid(tail), g2h_ref[...], g2l_ref[...])
    kk = k * kk_ref[...]
    norm = jnp.sqrt(_mm_sel_right(kk * kk, e64_ref[...], GROUP_SUM_TERMS))
    kk = kk / jnp.maximum(norm, 1e-12)
    k = k * (1.0 + (a - 1.0) * ka_ref[...])
    if gated:
        lo = _mm_split(v, v1h_ref[...], v1l_ref[...])
        mix = _sigmoid(v0_ref[...] + _mm_split(lo, v2h_ref[...], v2l_ref[...]))
        v = v + (vf_ref[...] - v) * mix
    else:
        vfirst_ref[...] = v
    bonus_ref[...] = _mm_sel_right(r * k * rk_ref[...], e64_ref[...], GROUP_SUM_TERMS) * v
    cs = _mm_sel_left(tri_ref[...], logw, 2)
    cs_end = jnp.concatenate(
        [jnp.broadcast_to(cs[c * RW_CHUNK + RW_CHUNK - 1:(c + 1) * RW_CHUNK, :], (RW_CHUNK, RW_WIDTH))
         for c in range(tm // RW_CHUNK)], axis=0)
    inv = jnp.exp(-cs)
    to_end = jnp.exp(cs_end - cs)
    b = kk * a
    rh_ref[...] = (r * jnp.exp(cs)).astype(BF16)
    ah_ref[...] = (-kk * jnp.exp(cs - logw)).astype(BF16)
    bh_ref[...] = (b * inv).astype(BF16)
    kh_ref[...] = (k * inv).astype(BF16)
    bt_ref[...] = (b * to_end).astype(BF16)
    kt_ref[...] = (k * to_end).astype(BF16)
    v_ref[...] = v.astype(BF16)
    gc_ref[...] = jnp.exp(cs_end)


def _pad_rows(w, start):
    return jnp.zeros((RW_TAIL, w.shape[1]), w.dtype).at[start:start + w.shape[0]].set(w)


def _hi_lo(w):
    hi = w.astype(BF16)
    return hi, (w - hi.astype(w.dtype)).astype(BF16)


def _rw_prep(z, p, v_first, seq, tm):
    t = z.shape[0]
    gated = v_first is not None
    idx = jnp.arange(tm)
    same_chunk = (idx[:, None] // RW_CHUNK) == (idx[None, :] // RW_CHUNK)
    tri = (same_chunk & (idx[None, :] <= idx[:, None])).astype(BF16)
    vec = lambda a: a[None, :]
    consts = [vec(p['mu']), vec(p['w0']), vec(p['a0']), vec(p['k_k']), vec(p['k_a']), vec(p['r_k'].reshape(-1)),
              *_hi_lo(_pad_rows(p['w2'], 0)), *_hi_lo(_pad_rows(p['a2'], RW_DECAY_LORA)),
              *_hi_lo(_pad_rows(p['g2'], RW_DECAY_LORA + RW_A_LORA)),
              _group_ones(RW_WIDTH, RW_HEAD), tri]
    if gated:
        consts += [vec(p['v0']), *_hi_lo(p['v1']), *_hi_lo(p['v2'])]
    row = lambda width: pl.BlockSpec((tm, width), lambda i: (i, 0))
    zp_spec = pl.BlockSpec((8, RW_SHIFT_WIDTH), lambda i: (jnp.maximum(i * (tm // 8) - 1, 0), 0))
    in_specs = [row(RW_SHIFT_WIDTH), zp_spec] + ([row(RW_WIDTH)] if gated else []) + [_const_spec(c.shape) for c in consts]
    args = [z, z] + ([v_first] if gated else []) + consts
    n_out = 10 if gated else 11
    dtypes = [BF16] * 7 + [F32] * 3 + ([] if gated else [F32])
    outs = pl.pallas_call(
        functools.partial(_rw_prep_kernel, tm=tm, seq=seq, gated=gated),
        grid=(t // tm,),
        in_specs=in_specs,
        out_specs=[row(RW_WIDTH)] * n_out,
        out_shape=[jax.ShapeDtypeStruct((t, RW_WIDTH), dt) for dt in dtypes],
        compiler_params=_cparams(1),
        name="rw_prep",
    )(*args)
    return outs


def _rw_intra_kernel(rh_ref, ah_ref, bh_ref, kh_ref, bt_ref, kt_ref, v_ref, gc_ref,
                     wr_ref, mb_ref, uloc_ref, oloc_ref, kvloc_ref, gcol_ref, *, n_sub):
    n = RW_CHUNK
    ri = lax.broadcasted_iota(jnp.int32, (n, n), 0)
    ci = lax.broadcasted_iota(jnp.int32, (n, n), 1)
    strict = ci < ri
    incl = ci <= ri
    eye = ci == ri
    eye_b = eye.astype(BF16)
    ones_b = jnp.ones((n, n), BF16)
    items = [(c, h) for c in range(n_sub) for h in range(RW_HEADS)]
    rows = lambda c: slice(c * n, (c + 1) * n)
    lanes = lambda h: slice(h * RW_HEAD, (h + 1) * RW_HEAD)
    ld = lambda ref: [ref[rows(c), lanes(h)] for c, h in items]
    rh, ah, bh, kh, bt, kt, v = (ld(r) for r in (rh_ref, ah_ref, bh_ref, kh_ref, bt_ref, kt_ref, v_ref))
    ar = [jnp.concatenate([a, r], axis=0) for a, r in zip(ah, rh)]
    gb = [_mm_nt(x, y) for x, y in zip(ar, bh)]
    gk = [_mm_nt(x, y) for x, y in zip(ar, kh)]
    l_ab = [jnp.where(strict, g[:n], 0.0) for g in gb]
    m_rb = [jnp.where(incl, g[n:], 0.0).astype(BF16) for g in gb]
    lm_k = [jnp.concatenate([jnp.where(strict, g[:n], 0.0), jnp.where(incl, g[n:], 0.0)], axis=0).astype(BF16) for g in gk]
    akv = [_mm(x, y) for x, y in zip(lm_k, v)]
    tinv = [jnp.where(eye, 1.0, x) for x in l_ab]
    l_ab = [x.astype(BF16) for x in l_ab]
    pw = [_mm(x, x) for x in l_ab]
    for _ in range(4):
        pw_b = [x.astype(BF16) for x in pw]
        res = [_mm(jnp.concatenate([p, t.astype(BF16)], axis=0), p) for p, t in zip(pw_b, tinv)]
        pw = [r[:n] for r in res]
        tinv = [t + r[n:] for t, r in zip(tinv, res)]
    tinv = [(t + _mm(t, p)).astype(BF16) for t, p in zip(tinv, pw)]
    w = [_mm(t, a) for t, a in zip(tinv, ah)]
    uloc = [_mm(t, x[:n]) for t, x in zip(tinv, akv)]
    kvloc = [_mm_tn(x, y) for x, y in zip(kt, v)]
    bt_t = [_mm_tn(x, eye_b) for x in bt]
    for i, (c, h) in enumerate(items):
        decay = jnp.where(eye, jnp.broadcast_to(gc_ref[(c + 1) * n - 1:(c + 1) * n, lanes(h)], (n, n)), 0.0)
        gcol_ref[rows(c), lanes(h)] = _mm_sel_right(decay, ones_b, 3)
        wr_ref[2 * c * n:(2 * c + 1) * n, lanes(h)] = w[i].astype(BF16)
        wr_ref[(2 * c + 1) * n:(2 * c + 2) * n, lanes(h)] = rh[i]
        mb_ref[2 * c * n:(2 * c + 1) * n, lanes(h)] = m_rb[i]
        mb_ref[(2 * c + 1) * n:(2 * c + 2) * n, lanes(h)] = bt_t[i].astype(BF16)
        uloc_ref[rows(c), lanes(h)] = uloc[i]
        oloc_ref[rows(c), lanes(h)] = akv[i][n:]
        kvloc_ref[rows(c), lanes(h)] = kvloc[i]


def _rw_seq_kernel(wr_ref, mb_ref, uloc_ref, oloc_ref, kvloc_ref, gcol_ref, y_ref, h_sc, *, n_batch):
    @pl.when(pl.program_id(0) == 0)
    def _():
        h_sc[...] = jnp.zeros(h_sc.shape, F32)

    n = RW_CHUNK
    blk = (lax.broadcasted_iota(jnp.int32, (RW_WIDTH, RW_WIDTH), 0) // RW_HEAD
           == lax.broadcasted_iota(jnp.int32, (RW_WIDTH, RW_WIDTH), 1) // RW_HEAD)
    block_diag = lambda x: jnp.where(blk, jnp.concatenate([x] * RW_HEADS, axis=0), 0.0).astype(BF16)
    batches = range(n_batch)
    h0 = [h_sc[b] for b in batches]
    s1 = [jnp.dot(wr_ref[b], block_diag(h0[b]), preferred_element_type=F32) for b in batches]
    u = [s1[b][:n] + uloc_ref[b] for b in batches]
    s2 = [jnp.dot(mb_ref[b], block_diag(u[b]), preferred_element_type=F32) for b in batches]
    for b in batches:
        y_ref[b] = s1[b][n:] + s2[b][:n] + oloc_ref[b]
        h_sc[b] = gcol_ref[b] * h0[b] + s2[b][n:] + kvloc_ref[b]


def _rw_scan(ops, n_batch, seq):
    t = n_batch * seq
    n_sub = 4
    rows = n_sub * RW_CHUNK
    spec = pl.BlockSpec((rows, RW_WIDTH), lambda i: (i, 0))
    spec2 = pl.BlockSpec((2 * rows, RW_WIDTH), lambda i: (i, 0))
    wr, mb, uloc, oloc, kvloc, gcol = pl.pallas_call(
        functools.partial(_rw_intra_kernel, n_sub=n_sub),
        grid=(t // rows,),
        in_specs=[spec] * 8,
        out_specs=[spec2, spec2, spec, spec, spec, spec],
        out_shape=[jax.ShapeDtypeStruct((2 * t, RW_WIDTH), BF16)] * 2 + [jax.ShapeDtypeStruct((t, RW_WIDTH), F32)] * 4,
        compiler_params=_cparams(1),
        name="rw_intra",
    )(*ops)
    n_chunk = seq // RW_CHUNK
    b3 = lambda a: a.reshape(n_batch, -1, RW_WIDTH)
    cspec = pl.BlockSpec((n_batch, RW_CHUNK, RW_WIDTH), lambda c: (0, c, 0))
    cspec2 = pl.BlockSpec((n_batch, 2 * RW_CHUNK, RW_WIDTH), lambda c: (0, c, 0))
    y = pl.pallas_call(
        functools.partial(_rw_seq_kernel, n_batch=n_batch),
        grid=(n_chunk,),
        in_specs=[cspec2, cspec2, cspec, cspec, cspec, cspec],
        out_specs=cspec,
        out_shape=jax.ShapeDtypeStruct((n_batch, seq, RW_WIDTH), F32),
        scratch_shapes=[pltpu.VMEM((n_batch, RW_HEAD, RW_WIDTH), F32)],
        compiler_params=_cparams(1),
        name="rw_seq",
    )(b3(wr), b3(mb), b3(uloc), b3(oloc), b3(kvloc), b3(gcol))
    return y.reshape(t, RW_WIDTH)


def _merge_kernel(x_ref, ys5_ref, us5_ref, yb_ref, yrw_ref, bonus_ref, g_ref, gl_ref, d_ref, wglu_ref, lng_ref, lnb_ref,
                  e64_ref, wb_ref, wout_ref, o_ref):
    ya = ys5_ref[...] + d_ref[...] * us5_ref[...]
    ya = 0.5 * ya * (1.0 + jnp.tanh(math.sqrt(2.0 / math.pi) * (ya + 0.044715 * (ya * ya * ya))))
    ya = ya * _sigmoid(_mm(ya, wglu_ref[...]))
    y = yrw_ref[...]
    mean = _mm_sel_right(y, e64_ref[...], GROUP_SUM_TERMS) * (1.0 / RW_HEAD)
    yc = y - mean
    var = _mm_sel_right(yc * yc, e64_ref[...], GROUP_SUM_TERMS) * (1.0 / RW_HEAD)
    yc = yc * lax.rsqrt(var + RW_LN_EPS) * lng_ref[...] + lnb_ref[...]
    yc = (yc + bonus_ref[...]) * g_ref[...]
    merged = jnp.zeros((x_ref.shape[0], D_MODEL), F32)
    for n, br in enumerate((ya, yb_ref[...], yc)):
        gate = _sigmoid(gl_ref[:, n * D_MODEL:(n + 1) * D_MODEL])
        merged = merged + gate * _mm(br, wb_ref[n])
    o_ref[...] = x_ref[...] + _mm(merged, wout_ref[...])


def _merge(x2, ys5, us5, yb, yrw, bonus, g, gl, s5_d, w_glu, ln_g, ln_b, w_branch, w_out, tm):
    t = x2.shape[0]
    consts = [s5_d[None, :], w_glu.astype(BF16), ln_g[None, :], ln_b[None, :], _group_ones(RW_WIDTH, RW_HEAD),
              w_branch.astype(BF16), w_out.astype(BF16)]
    row = lambda width: pl.BlockSpec((tm, width), lambda i: (i, 0))
    widths = [D_MODEL, S5_WIDTH, S5_WIDTH, DA_WIDTH, RW_WIDTH, RW_WIDTH, RW_WIDTH, N_BRANCH * D_MODEL]
    return pl.pallas_call(
        _merge_kernel,
        grid=(t // tm,),
        in_specs=[row(wd) for wd in widths] + [_const_spec(c.shape) for c in consts],
        out_specs=row(D_MODEL),
        out_shape=jax.ShapeDtypeStruct((t, D_MODEL), F32),
        compiler_params=_cparams(1),
        name="merge",
    )(x2, ys5, us5, yb, yrw, bonus, g, gl, *consts)


def _ffn_kernel(x_ref, g_ref, win_ref, wout_ref, o_ref):
    x = x_ref[...]
    h = (x * lax.rsqrt(jnp.mean(x * x, axis=-1, keepdims=True) + NORM_EPS) * g_ref[...]).astype(BF16)
    gate = jnp.dot(h, win_ref[:, :D_FF], preferred_element_type=F32)
    up = jnp.dot(h, win_ref[:, D_FF:], preferred_element_type=F32)
    act = gate * _sigmoid(gate) * up
    o_ref[...] = x + _mm(act, wout_ref[...])


def _ffn(x2, g, w_in, w_out, tm):
    t = x2.shape[0]
    consts = [g[None, :], w_in.astype(BF16), w_out.astype(BF16)]
    row = pl.BlockSpec((tm, D_MODEL), lambda i: (i, 0))
    return pl.pallas_call(
        _ffn_kernel,
        grid=(t // tm,),
        in_specs=[row] + [_const_spec(c.shape) for c in consts],
        out_specs=row,
        out_shape=jax.ShapeDtypeStruct((t, D_MODEL), F32),
        compiler_params=_cparams(1),
        name="ffn",
    )(x2, *consts)


def _layer(x2, i, n_batch, seq, p, v_first, tm):
    us5, q, k, v, zrw, gl = _in_proj(x2, p['norm1_g'], p['w_in'], p['da_q_gain'], p['da_k_gain'], tm)
    n_double = max(1, (seq // S5_CHUNK - 1).bit_length())
    tables = _s5_tables(p['s5_lambda_re'], p['s5_lambda_im'], p['s5_log_dt'], p['s5_b_re'], p['s5_b_im'],
                        p['s5_c_re'], p['s5_c_im'], n_double)
    ys5 = _s5_scan(us5, tables, n_batch, seq)
    lambda_init = 0.8 - 0.6 * math.exp(-0.3 * i)
    yb = _diff_attn(q, k, v, p['da_lambda'], p['da_subln_g'], lambda_init, n_batch, seq)
    rw = {name[3:]: val for name, val in p.items() if name.startswith('rw_')}
    outs = _rw_prep(zrw, rw, v_first, seq, min(RW_PREP_TILE, seq))
    if v_first is None:
        v_first = outs[10]
    yrw = _rw_scan(outs[:8], n_batch, seq)
    x2 = _merge(x2, ys5, us5, yb, yrw, outs[8], outs[9], gl, p['s5_d'], p['s5_w_glu'], p['rw_ln_g'], p['rw_ln_b'],
                p['w_branch'], p['w_out'], tm)
    x2 = _ffn(x2, p['norm2_g'], p['w_ffn_in'], p['w_ffn_out'], tm)
    return x2, v_first


def kernel(x, norm1_g, w_in, s5_lambda_re, s5_lambda_im, s5_log_dt, s5_b_re, s5_b_im, s5_c_re, s5_c_im, s5_d, s5_w_glu, da_q_gain, da_k_gain, da_lambda, da_subln_g, rw_mu, rw_w0, rw_w2, rw_a0, rw_a2, rw_g2, rw_k_k, rw_k_a, rw_r_k, rw_ln_g, rw_ln_b, rw_v0, rw_v1, rw_v2, w_branch, w_out, norm2_g, w_ffn_in, w_ffn_out):
    n_batch, seq, _ = x.shape
    per_layer = dict(norm1_g=norm1_g, w_in=w_in, s5_lambda_re=s5_lambda_re, s5_lambda_im=s5_lambda_im,
                     s5_log_dt=s5_log_dt, s5_b_re=s5_b_re, s5_b_im=s5_b_im, s5_c_re=s5_c_re, s5_c_im=s5_c_im,
                     s5_d=s5_d, s5_w_glu=s5_w_glu, da_q_gain=da_q_gain, da_k_gain=da_k_gain, da_lambda=da_lambda,
                     da_subln_g=da_subln_g, rw_mu=rw_mu, rw_w0=rw_w0, rw_w2=rw_w2, rw_a0=rw_a0, rw_a2=rw_a2,
                     rw_g2=rw_g2, rw_k_k=rw_k_k, rw_k_a=rw_k_a, rw_r_k=rw_r_k, rw_ln_g=rw_ln_g, rw_ln_b=rw_ln_b,
                     w_branch=w_branch, w_out=w_out, norm2_g=norm2_g, w_ffn_in=w_ffn_in, w_ffn_out=w_ffn_out)
    tm = min(TOKEN_TILE, seq)
    x2 = x.reshape(n_batch * seq, D_MODEL)
    v_first = None
    for i in range(w_in.shape[0]):
        p = {name: val[i] for name, val in per_layer.items()}
        if i > 0:
            p.update(rw_v0=rw_v0[i - 1], rw_v1=rw_v1[i - 1], rw_v2=rw_v2[i - 1])
        x2, v_first = _layer(x2, i, n_batch, seq, p, v_first, tm)
    return x2.reshape(x.shape)
```

```python
import functools
import math

import jax
import jax.numpy as jnp
from jax import lax
from jax.experimental import pallas as pl
from jax.experimental.pallas import tpu as pltpu

F32 = jnp.float32
BF16 = jnp.bfloat16

D_MODEL = 1024
DEPTH = 2
S5_WIDTH = 512
S5_GROUP = 16
S5_GROUPS = S5_WIDTH // S5_GROUP
S5_STATE = 64
S5_CHUNK = 16
S5_SLAB = 8
DA_HEADS = 4
DA_HEAD_DIM = 64
DA_V_DIM = 2 * DA_HEAD_DIM
DA_WIDTH = DA_HEADS * DA_V_DIM
DA_SUBLN_EPS = 1e-5
RW_HEAD = 64
RW_WIDTH = 512
RW_HEADS = RW_WIDTH // RW_HEAD
RW_DECAY_LORA = 32
RW_A_LORA = 32
RW_G_LORA = 96
RW_TAIL = RW_DECAY_LORA + RW_A_LORA + RW_G_LORA
RW_SHIFT_WIDTH = 3 * RW_WIDTH + RW_TAIL
RW_LN_EPS = 64e-5
RW_CHUNK = 64
N_BRANCH = 3
D_FF = 2816
NORM_EPS = 1e-6
NEG_BIG = -1e30
ATTN_TQ = 1024
ATTN_STRIP = 256
TOKEN_TILE = 512
GROUP_SUM_TERMS = 1
RW_PREP_TILE = 256

VMEM_LIMIT = 56 * 1024 * 1024


def _cparams(n_axes):
    return pltpu.CompilerParams(dimension_semantics=("arbitrary",) * n_axes, vmem_limit_bytes=VMEM_LIMIT)


def _const_spec(shape):
    nd = len(shape)
    return pl.BlockSpec(shape, lambda *_: (0,) * nd, pipeline_mode=pl.Buffered(1))


def _mm(a, b):
    return jnp.dot(a.astype(BF16), b.astype(BF16), preferred_element_type=F32)


def _mm_nt(a, b):
    return lax.dot_general(a.astype(BF16), b.astype(BF16), (((1,), (1,)), ((), ())), preferred_element_type=F32)


def _mm_tn(a, b):
    return lax.dot_general(a.astype(BF16), b.astype(BF16), (((0,), (0,)), ((), ())), preferred_element_type=F32)


def _split2(x):
    hi = x.astype(BF16)
    lo = (x - hi.astype(F32)).astype(BF16)
    return hi, lo


def _split3(x):
    hi = x.astype(BF16)
    r = x - hi.astype(F32)
    mid = r.astype(BF16)
    lo = (r - mid.astype(F32)).astype(BF16)
    return hi, mid, lo


def _bf16_terms(x, terms):
    return {1: lambda: (x.astype(BF16),), 2: lambda: _split2(x), 3: lambda: _split3(x)}[terms]()


def _mm_sel_right(x, sel, terms):
    return sum(jnp.dot(p, sel, preferred_element_type=F32) for p in _bf16_terms(x, terms))


def _mm_sel_left(sel, x, terms):
    return sum(jnp.dot(sel, p, preferred_element_type=F32) for p in _bf16_terms(x, terms))


def _mm_split(x, w_hi, w_lo):
    x_hi, x_lo = _split2(x)
    d = lambda p, q: jnp.dot(p, q, preferred_element_type=F32)
    return d(x_hi, w_hi) + d(x_hi, w_lo) + d(x_lo, w_hi)


def _sigmoid(x):
    return 1.0 / (1.0 + jnp.exp(-x))


def _group_ones(width, group):
    r = jnp.arange(width) // group
    return (r[:, None] == r[None, :]).astype(BF16)


def _in_proj_kernel(x_ref, g_ref, ws5_ref, wq_ref, wk_ref, wvt_ref, wrw_ref, wgate_ref, e64_ref, qg_ref, kg_ref,
                    us5_ref, q_ref, k_ref, vt_ref, zrw_ref, gl_ref):
    x = x_ref[...]
    h = x * lax.rsqrt(jnp.mean(x * x, axis=-1, keepdims=True) + NORM_EPS) * g_ref[...]
    h = h.astype(BF16)
    dot = lambda w_ref: jnp.dot(h, w_ref[...], preferred_element_type=F32)
    us5_ref[...] = dot(ws5_ref)

    def qk_norm(t, gain_ref):
        ms = _mm_sel_right(t * t, e64_ref[...], GROUP_SUM_TERMS) * (1.0 / DA_HEAD_DIM)
        return t * lax.rsqrt(ms + NORM_EPS) * gain_ref[...]

    q_ref[...] = qk_norm(dot(wq_ref), qg_ref).astype(BF16)
    k_ref[...] = qk_norm(dot(wk_ref), kg_ref).astype(BF16)
    vt_ref[0] = lax.dot_general(wvt_ref[...], h, (((1,), (1,)), ((), ())), preferred_element_type=F32).astype(BF16)
    zrw_ref[...] = dot(wrw_ref)
    gl_ref[...] = dot(wgate_ref)


def _in_proj(x2, g, w, q_gain, k_gain, tm):
    t = x2.shape[0]
    w = w.astype(BF16)
    o = 0
    parts = []
    for width in (S5_WIDTH, DA_WIDTH, DA_WIDTH, DA_WIDTH, RW_SHIFT_WIDTH, N_BRANCH * D_MODEL):
        parts.append(w[:, o:o + width])
        o += width
    parts[3] = parts[3].T
    e64 = _group_ones(DA_WIDTH, DA_HEAD_DIM)
    qg = jnp.tile(q_gain.reshape(-1), DA_HEADS)[None, :] * (DA_HEAD_DIM ** -0.5 * math.log2(math.e))
    kg = jnp.tile(k_gain.reshape(-1), DA_HEADS)[None, :]
    row = lambda width: pl.BlockSpec((tm, width), lambda i: (i, 0))
    consts = [g[None, :]] + parts + [e64, qg, kg]
    out_specs = [row(S5_WIDTH), row(DA_WIDTH), row(DA_WIDTH), pl.BlockSpec((1, DA_WIDTH, tm), lambda i: (i, 0, 0)),
                 row(RW_SHIFT_WIDTH), row(N_BRANCH * D_MODEL)]
    out_shape = [jax.ShapeDtypeStruct((t, S5_WIDTH), F32), jax.ShapeDtypeStruct((t, DA_WIDTH), BF16),
                 jax.ShapeDtypeStruct((t, DA_WIDTH), BF16), jax.ShapeDtypeStruct((t // tm, DA_WIDTH, tm), BF16),
                 jax.ShapeDtypeStruct((t, RW_SHIFT_WIDTH), F32), jax.ShapeDtypeStruct((t, N_BRANCH * D_MODEL), F32)]
    return pl.pallas_call(
        _in_proj_kernel,
        grid=(t // tm,),
        in_specs=[row(D_MODEL)] + [_const_spec(c.shape) for c in consts],
        out_specs=out_specs,
        out_shape=out_shape,
        compiler_params=_cparams(1),
        name="in_proj",
    )(x2, *consts)


def _s5_tables(lam_re, lam_im, log_dt, b_re, b_im, c_re, c_im, n_double):
    n = S5_CHUNK
    dt = jnp.exp(log_dt)[:, None]
    den = lam_re * lam_re + lam_im * lam_im

    def a_pow(tau):
        mag = jnp.exp(lam_re * dt * tau)
        return mag * jnp.cos(lam_im * dt * tau), mag * jnp.sin(lam_im * dt * tau)

    ar, ai = a_pow(1.0)
    nr = ar - 1.0
    kr = (nr * lam_re + ai * lam_im) / den
    ki = (ai * lam_re - nr * lam_im) / den
    bbar_r = kr[..., None] * b_re - ki[..., None] * b_im
    bbar_i = kr[..., None] * b_im + ki[..., None] * b_re
    taus = jnp.arange(n + 1, dtype=lam_re.dtype)
    pr, pi = jax.vmap(a_pow)(taus)
    hp = lax.Precision.HIGHEST
    abr = pr[..., None] * bbar_r[None] - pi[..., None] * bbar_i[None]
    abi = pr[..., None] * bbar_i[None] + pi[..., None] * bbar_r[None]
    ktau = (jnp.einsum('gdp,ngpc->ngdc', c_re, abr, precision=hp)
            - jnp.einsum('gdp,ngpc->ngdc', c_im, abi, precision=hp))
    g, q, c, p2 = lam_re.shape[0], S5_SLAB, S5_GROUP, 2 * S5_STATE
    j = g // q
    eye = jnp.eye(q, dtype=ktau.dtype)
    kl = ktau[:n].transpose(1, 0, 3, 2).reshape(j, q, n, c, c).transpose(0, 2, 1, 3, 4)
    klag = (kl[:, :, :, :, None, :] * eye[None, None, :, None, :, None]).reshape(j, n, q * c, q * c)
    rev = n - 1 - jnp.arange(n)
    b_end = jnp.concatenate([abr[rev], abi[rev]], axis=2)
    bend = b_end.reshape(n, j, q, p2, c).transpose(1, 0, 2, 4, 3).reshape(j, n, q * c, p2)
    cr = c_re[None] * pr[1:, :, None, :] - c_im[None] * pi[1:, :, None, :]
    ci = c_re[None] * pi[1:, :, None, :] + c_im[None] * pr[1:, :, None, :]
    c_in = jnp.concatenate([cr, -ci], axis=3)
    cin = c_in.reshape(n, j, q, c, p2).transpose(1, 0, 4, 2, 3).reshape(j, n, p2, q * c)
    rows = []
    for k in range(n_double):
        sr, si = a_pow(float(n * 2 ** k))
        rows += [jnp.concatenate([sr, sr], axis=1), jnp.concatenate([-si, si], axis=1)]
    step = jnp.stack(rows, axis=1)
    step = step.reshape(j, q, 2 * n_double, p2).transpose(0, 2, 1, 3).reshape(j, 2 * n_double, q * p2)
    return klag.astype(BF16), bend.astype(BF16), cin.astype(BF16), step


def _s5_kernel(u_ref, klag_ref, bend_ref, cin_ref, st_ref, y_ref, tt_sc, bb_sc, cc_sc, *, n_double):
    n, w = S5_CHUNK, S5_SLAB * S5_GROUP

    @pl.when(pl.program_id(1) == 0)
    def _():
        zero = jnp.zeros((w, w), BF16)
        row_q = lax.broadcasted_iota(jnp.int32, (w, w), 0) // S5_GROUP
        lane_q = lax.broadcasted_iota(jnp.int32, (w, w), 1) // S5_GROUP
        for s in range(n):
            for t in range(n):
                tt_sc[s * w:(s + 1) * w, t * w:(t + 1) * w] = klag_ref[0, t - s] if t >= s else zero
            b_blk, c_blk = bend_ref[0, s], cin_ref[0, s]
            for q in range(S5_SLAB):
                bb_sc[s * w:(s + 1) * w, q * w:(q + 1) * w] = jnp.where(row_q == q, b_blk, zero)
                cc_sc[q * w:(q + 1) * w, s * w:(s + 1) * w] = jnp.where(lane_q == q, c_blk, zero)

    u = jnp.concatenate([u_ref[:, s, :].astype(BF16) for s in range(n)], axis=1)
    x = jnp.dot(u, bb_sc[...], preferred_element_type=F32)
    row = lax.broadcasted_iota(jnp.int32, x.shape, 0)
    n_chunk, width = x.shape

    def swap(a):
        is_re = lax.broadcasted_iota(jnp.int32, a.shape, 1) % (2 * S5_STATE) < S5_STATE
        return jnp.where(is_re, pltpu.roll(a, width - S5_STATE, 1), pltpu.roll(a, S5_STATE, 1))

    for k in range(n_double):
        s, c1, c2 = 2 ** k, st_ref[0, 2 * k:2 * k + 1, :], st_ref[0, 2 * k + 1:2 * k + 2, :]
        if s % 8 == 0 and s < n_chunk:
            lo = x[:n_chunk - s]
            x = jnp.concatenate([x[:s], x[s:] + c1 * lo + c2 * swap(lo)], axis=0)
        else:
            sh = jnp.where(row >= s, pltpu.roll(x, s, 0), 0.0)
            x = x + c1 * sh + c2 * swap(sh)
    xin = jnp.where(row >= 1, pltpu.roll(x, 1, 0), 0.0).astype(BF16)
    y_in = jnp.dot(xin, cc_sc[...], preferred_element_type=F32)
    for t0 in range(0, n, 2):
        rows_used = (t0 + 2) * w
        y_loc = jnp.dot(u[:, :rows_used], tt_sc[:rows_used, t0 * w:(t0 + 2) * w], preferred_element_type=F32)
        for t in (t0, t0 + 1):
            y_ref[:, t, :] = y_loc[:, (t - t0) * w:(t - t0 + 1) * w] + y_in[:, t * w:(t + 1) * w]


def _s5_scan(u, tables, n_batch, seq):
    klag, bend, cin, st = tables
    n_chunk = seq // S5_CHUNK
    n_slab = klag.shape[0]
    w = S5_SLAB * S5_GROUP
    u3 = u.reshape(n_batch * n_chunk, S5_CHUNK, S5_WIDTH)
    per_slab = lambda a: pl.BlockSpec((1,) + a.shape[1:], lambda j, b: (j,) + (0,) * (a.ndim - 1))
    io_spec = pl.BlockSpec((n_chunk, S5_CHUNK, w), lambda j, b: (b, 0, j))
    y = pl.pallas_call(
        functools.partial(_s5_kernel, n_double=st.shape[1] // 2),
        grid=(n_slab, n_batch),
        in_specs=[io_spec, per_slab(klag), per_slab(bend), per_slab(cin), per_slab(st)],
        out_specs=io_spec,
        out_shape=jax.ShapeDtypeStruct(u3.shape, F32),
        scratch_shapes=[pltpu.VMEM((S5_CHUNK * w, S5_CHUNK * w), BF16),
                        pltpu.VMEM((S5_CHUNK * w, S5_SLAB * 2 * S5_STATE), BF16),
                        pltpu.VMEM((S5_SLAB * 2 * S5_STATE, S5_CHUNK * w), BF16)],
        compiler_params=_cparams(2),
        name="s5_chunk",
    )(u3, klag, bend, cin, st)
    return y.reshape(u.shape)


def _attn_kernel(lam_ref, sg_ref, q_ref, k_ref, vt_ref, o_ref, m_sc, acc_sc, *, tq, lambda_init):
    tk = vt_ref.shape[2]
    qi = pl.program_id(2)
    q = q_ref[...]
    lane = lax.broadcasted_iota(jnp.int32, q.shape, 1)
    zero = jnp.zeros_like(q)
    q2 = jnp.concatenate([jnp.where(lane < DA_HEAD_DIM, q, zero), jnp.where(lane >= DA_HEAD_DIM, q, zero)], axis=0)
    m_sc[...] = jnp.full(m_sc.shape, NEG_BIG, F32)
    acc_sc[...] = jnp.zeros(acc_sc.shape, F32)

    n_strip = 2 * tq // ATTN_STRIP
    n_sub = tq // tk
    ones_rows = jnp.ones((16, tk), BF16)

    def step(j, masked):
        kb = k_ref[pl.ds(pl.multiple_of(j * tq, tq), tq), :]
        vtb = [jnp.concatenate([vt_ref[j * n_sub + i], ones_rows], axis=0) for i in range(n_sub)]
        n_key = [(c * ATTN_STRIP) % tq + ATTN_STRIP if masked else tq for c in range(n_strip)]
        s = [lax.dot_general(kb[:n_key[c]], q2[c * ATTN_STRIP:(c + 1) * ATTN_STRIP], (((1,), (1,)), ((), ())),
                             preferred_element_type=F32) for c in range(n_strip)]
        for c in range(n_strip):
            cs = slice(c * ATTN_STRIP, (c + 1) * ATTN_STRIP)
            sc = s[c]
            if masked:
                key = lax.broadcasted_iota(jnp.int32, sc.shape, 0)
                col = lax.broadcasted_iota(jnp.int32, sc.shape, 1)
                sc = jnp.where(key <= col + (c * ATTN_STRIP) % tq, sc, NEG_BIG)
            m_prev = m_sc[:, cs]
            m_next = jnp.maximum(m_prev, jnp.max(sc, axis=0, keepdims=True))
            alpha = jnp.exp2(m_prev - m_next)
            p = jnp.exp2(sc - m_next).astype(BF16)
            m_sc[:, cs] = m_next
            pv = sum(jnp.dot(vtb[i][:, :min(tk, n_key[c] - i * tk)], p[i * tk:min((i + 1) * tk, n_key[c])],
                             preferred_element_type=F32) for i in range(n_sub) if i * tk < n_key[c])
            acc_sc[:, cs] = alpha * acc_sc[:, cs] + pv

    def body(jj, carry):
        step(2 * jj, False)
        step(2 * jj + 1, False)
        return carry

    lax.fori_loop(0, qi // 2, body, 0)

    @pl.when(qi % 2 == 1)
    def _():
        step(qi - 1, False)

    step(qi, True)

    lv = lam_ref[...]
    lam = (jnp.exp(jnp.sum(lv[0:1] * lv[1:2], axis=1, keepdims=True))
           - jnp.exp(jnp.sum(lv[2:3] * lv[3:4], axis=1, keepdims=True)) + lambda_init)
    o_all = acc_sc[:DA_V_DIM, :] / acc_sc[DA_V_DIM:DA_V_DIM + 1, :]
    o = (o_all[:, :tq] - lam * o_all[:, tq:]).T
    o = o * lax.rsqrt(jnp.mean(o * o, axis=-1, keepdims=True) + DA_SUBLN_EPS) * sg_ref[...] * (1.0 - lambda_init)
    o_ref[...] = o.astype(o_ref.dtype)


def _diff_attn(q, k, vt, lam_vecs, subln_g, lambda_init, n_batch, seq):
    tk = vt.shape[2]
    tq = min(ATTN_TQ, seq)
    nq = seq // tq
    q_spec = pl.BlockSpec((tq, DA_V_DIM), lambda b, h, i: (b * nq + i, h))
    k_spec = pl.BlockSpec((seq, DA_V_DIM), lambda b, h, i: (b, h))
    vt_spec = pl.BlockSpec((seq // tk, DA_V_DIM, tk), lambda b, h, i: (b, h, 0))
    return pl.pallas_call(
        functools.partial(_attn_kernel, tq=tq, lambda_init=lambda_init),
        grid=(n_batch, DA_HEADS, nq),
        in_specs=[_const_spec(lam_vecs.shape), _const_spec((1, DA_V_DIM)), q_spec, k_spec, vt_spec],
        out_specs=q_spec,
        out_shape=jax.ShapeDtypeStruct((n_batch * seq, DA_WIDTH), BF16),
        scratch_shapes=[pltpu.VMEM((1, 2 * tq), F32), pltpu.VMEM((DA_V_DIM + 16, 2 * tq), F32)],
        compiler_params=_cparams(3),
        name="diff_attn",
    )(lam_vecs, subln_g[None, :], q, k, vt)


def _rw_prep_kernel(*refs, tm, seq, gated):
    if gated:
        (z_ref, zp_ref, vf_ref, mu_ref, w0_ref, a0_ref, kk_ref, ka_ref, rk_ref, w2h_ref, w2l_ref, a2h_ref, a2l_ref,
         g2h_ref, g2l_ref, e64_ref, tri_ref, v0_ref, v1h_ref, v1l_ref, v2h_ref, v2l_ref,
         rh_ref, ah_ref, bh_ref, kh_ref, bt_ref, kt_ref, v_ref, gc_ref, bonus_ref, g_ref) = refs
    else:
        (z_ref, zp_ref, mu_ref, w0_ref, a0_ref, kk_ref, ka_ref, rk_ref, w2h_ref, w2l_ref, a2h_ref, a2l_ref,
         g2h_ref, g2l_ref, e64_ref, tri_ref,
         rh_ref, ah_ref, bh_ref, kh_ref, bt_ref, kt_ref, v_ref, gc_ref, bonus_ref, g_ref, vfirst_ref) = refs
    i = pl.program_id(0)
    z = z_ref[...]
    first = jnp.where((i * tm) % seq == 0, 0.0, 1.0) * zp_ref[7:8, :]
    row = lax.broadcasted_iota(jnp.int32, z.shape, 0)
    prev = jnp.where(row == 0, first, pltpu.roll(z, 1, 0))
    zf = z + (prev - z) * mu_ref[...]
    r = zf[:, :RW_WIDTH]
    k = zf[:, RW_WIDTH:2 * RW_WIDTH]
    v = zf[:, 2 * RW_WIDTH:3 * RW_WIDTH]
    tail = zf[:, 3 * RW_WIDTH:]
    w_in = w0_ref[...] + _mm_split(jnp.tanh(tail), w2h_ref[...], w2l_ref[...])
    softplus = jnp.maximum(-w_in, 0.0) + jnp.log(1.0 + jnp.exp(-jnp.abs(w_in)))
    logw = -jnp.exp(-softplus - 0.5)
    a = _sigmoid(a0_ref[...] + _mm_split(tail, a2h_ref[...], a2l_ref[...]))
    g_ref[...] = _mm_split(_sigmoid(tail), g2h_ref[...], g2l_ref[...])
    kk = k * kk_ref[...]
    norm = jnp.sqrt(_mm_sel_right(kk * kk, e64_ref[...], GROUP_SUM_TERMS))
    kk = kk / jnp.maximum(norm, 1e-12)
    k = k * (1.0 + (a - 1.0) * ka_ref[...])
    if gated:
        lo = _mm_split(v, v1h_ref[...], v1l_ref[...])
        mix = _sigmoid(v0_ref[...] + _mm_split(lo, v2h_ref[...], v2l_ref[...]))
        v = v + (vf_ref[...] - v) * mix
    else:
        vfirst_ref[...] = v
    bonus_ref[...] = _mm_sel_right(r * k * rk_ref[...], e64_ref[...], GROUP_SUM_TERMS) * v
    cs = _mm_sel_left(tri_ref[...], logw, 2)
    cs_end = jnp.concatenate(
        [jnp.broadcast_to(cs[c * RW_CHUNK + RW_CHUNK - 1:(c + 1) * RW_CHUNK, :], (RW_CHUNK, RW_WIDTH))
         for c in range(tm // RW_CHUNK)], axis=0)
    inv = jnp.exp(-cs)
    to_end = jnp.exp(cs_end - cs)
    b = kk * a
    rh_ref[...] = (r * jnp.exp(cs)).astype(BF16)
    ah_ref[...] = (-kk * jnp.exp(cs - logw)).astype(BF16)
    bh_ref[...] = (b * inv).astype(BF16)
    kh_ref[...] = (k * inv).astype(BF16)
    bt_ref[...] = (b * to_end).astype(BF16)
    kt_ref[...] = (k * to_end).astype(BF16)
    v_ref[...] = v.astype(BF16)
    gc_ref[...] = jnp.exp(cs_end)


def _pad_rows(w, start):
    return jnp.zeros((RW_TAIL, w.shape[1]), w.dtype).at[start:start + w.shape[0]].set(w)


def _hi_lo(w):
    hi = w.astype(BF16)
    return hi, (w - hi.astype(w.dtype)).astype(BF16)


def _rw_prep(z, p, v_first, seq, tm):
    t = z.shape[0]
    gated = v_first is not None
    idx = jnp.arange(tm)
    same_chunk = (idx[:, None] // RW_CHUNK) == (idx[None, :] // RW_CHUNK)
    tri = (same_chunk & (idx[None, :] <= idx[:, None])).astype(BF16)
    vec = lambda a: a[None, :]
    consts = [vec(p['mu']), vec(p['w0']), vec(p['a0']), vec(p['k_k']), vec(p['k_a']), vec(p['r_k'].reshape(-1)),
              *_hi_lo(_pad_rows(p['w2'], 0)), *_hi_lo(_pad_rows(p['a2'], RW_DECAY_LORA)),
              *_hi_lo(_pad_rows(p['g2'], RW_DECAY_LORA + RW_A_LORA)),
              _group_ones(RW_WIDTH, RW_HEAD), tri]
    if gated:
        consts += [vec(p['v0']), *_hi_lo(p['v1']), *_hi_lo(p['v2'])]
    row = lambda width: pl.BlockSpec((tm, width), lambda i: (i, 0))
    zp_spec = pl.BlockSpec((8, RW_SHIFT_WIDTH), lambda i: (jnp.maximum(i * (tm // 8) - 1, 0), 0))
    in_specs = [row(RW_SHIFT_WIDTH), zp_spec] + ([row(RW_WIDTH)] if gated else []) + [_const_spec(c.shape) for c in consts]
    args = [z, z] + ([v_first] if gated else []) + consts
    n_out = 10 if gated else 11
    dtypes = [BF16] * 7 + [F32] * 3 + ([] if gated else [F32])
    outs = pl.pallas_call(
        functools.partial(_rw_prep_kernel, tm=tm, seq=seq, gated=gated),
        grid=(t // tm,),
        in_specs=in_specs,
        out_specs=[row(RW_WIDTH)] * n_out,
        out_shape=[jax.ShapeDtypeStruct((t, RW_WIDTH), dt) for dt in dtypes],
        compiler_params=_cparams(1),
        name="rw_prep",
    )(*args)
    return outs


def _rw_intra_kernel(rh_ref, ah_ref, bh_ref, kh_ref, bt_ref, kt_ref, v_ref, gc_ref,
                     wr_ref, mb_ref, uloc_ref, oloc_ref, kvloc_ref, gcol_ref, *, n_sub):
    n = RW_CHUNK
    ri = lax.broadcasted_iota(jnp.int32, (n, n), 0)
    ci = lax.broadcasted_iota(jnp.int32, (n, n), 1)
    strict = ci < ri
    incl = ci <= ri
    eye = ci == ri
    eye_b = eye.astype(BF16)
    ones_b = jnp.ones((n, n), BF16)
    items = [(c, h) for c in range(n_sub) for h in range(RW_HEADS)]
    rows = lambda c: slice(c * n, (c + 1) * n)
    lanes = lambda h: slice(h * RW_HEAD, (h + 1) * RW_HEAD)
    ld = lambda ref: [ref[rows(c), lanes(h)] for c, h in items]
    rh, ah, bh, kh, bt, kt, v = (ld(r) for r in (rh_ref, ah_ref, bh_ref, kh_ref, bt_ref, kt_ref, v_ref))
    ar = [jnp.concatenate([a, r], axis=0) for a, r in zip(ah, rh)]
    gb = [_mm_nt(x, y) for x, y in zip(ar, bh)]
    gk = [_mm_nt(x, y) for x, y in zip(ar, kh)]
    l_ab = [jnp.where(strict, g[:n], 0.0) for g in gb]
    m_rb = [jnp.where(incl, g[n:], 0.0).astype(BF16) for g in gb]
    lm_k = [jnp.concatenate([jnp.where(strict, g[:n], 0.0), jnp.where(incl, g[n:], 0.0)], axis=0).astype(BF16) for g in gk]
    akv = [_mm(x, y) for x, y in zip(lm_k, v)]
    tinv = [jnp.where(eye, 1.0, x) for x in l_ab]
    l_ab = [x.astype(BF16) for x in l_ab]
    pw = [_mm(x, x) for x in l_ab]
    for _ in range(4):
        pw_b = [x.astype(BF16) for x in pw]
        res = [_mm(jnp.concatenate([p, t.astype(BF16)], axis=0), p) for p, t in zip(pw_b, tinv)]
        pw = [r[:n] for r in res]
        tinv = [t + r[n:] for t, r in zip(tinv, res)]
    tinv = [(t + _mm(t, p)).astype(BF16) for t, p in zip(tinv, pw)]
    w = [_mm(t, a) for t, a in zip(tinv, ah)]
    uloc = [_mm(t, x[:n]) for t, x in zip(tinv, akv)]
    kvloc = [_mm_tn(x, y) for x, y in zip(kt, v)]
    bt_t = [_mm_tn(x, eye_b) for x in bt]
    for i, (c, h) in enumerate(items):
        decay = jnp.where(eye, jnp.broadcast_to(gc_ref[(c + 1) * n - 1:(c + 1) * n, lanes(h)], (n, n)), 0.0)
        gcol_ref[rows(c), lanes(h)] = _mm_sel_right(decay, ones_b, 3)
        wr_ref[2 * c * n:(2 * c + 1) * n, lanes(h)] = w[i].astype(BF16)
        wr_ref[(2 * c + 1) * n:(2 * c + 2) * n, lanes(h)] = rh[i]
        mb_ref[2 * c * n:(2 * c + 1) * n, lanes(h)] = m_rb[i]
        mb_ref[(2 * c + 1) * n:(2 * c + 2) * n, lanes(h)] = bt_t[i].astype(BF16)
        uloc_ref[rows(c), lanes(h)] = uloc[i]
        oloc_ref[rows(c), lanes(h)] = akv[i][n:]
        kvloc_ref[rows(c), lanes(h)] = kvloc[i]


def _rw_seq_kernel(wr_ref, mb_ref, uloc_ref, oloc_ref, kvloc_ref, gcol_ref, y_ref, h_sc, *, n_batch):
    @pl.when(pl.program_id(0) == 0)
    def _():
        h_sc[...] = jnp.zeros(h_sc.shape, F32)

    n = RW_CHUNK
    blk = (lax.broadcasted_iota(jnp.int32, (RW_WIDTH, RW_WIDTH), 0) // RW_HEAD
           == lax.broadcasted_iota(jnp.int32, (RW_WIDTH, RW_WIDTH), 1) // RW_HEAD)
    block_diag = lambda x: jnp.where(blk, jnp.concatenate([x] * RW_HEADS, axis=0), 0.0).astype(BF16)
    batches = range(n_batch)
    h0 = [h_sc[b] for b in batches]
    s1 = [jnp.dot(wr_ref[b], block_diag(h0[b]), preferred_element_type=F32) for b in batches]
    u = [s1[b][:n] + uloc_ref[b] for b in batches]
    s2 = [jnp.dot(mb_ref[b], block_diag(u[b]), preferred_element_type=F32) for b in batches]
    for b in batches:
        y_ref[b] = s1[b][n:] + s2[b][:n] + oloc_ref[b]
        h_sc[b] = gcol_ref[b] * h0[b] + s2[b][n:] + kvloc_ref[b]


def _rw_scan(ops, n_batch, seq):
    t = n_batch * seq
    n_sub = 4
    rows = n_sub * RW_CHUNK
    spec = pl.BlockSpec((rows, RW_WIDTH), lambda i: (i, 0))
    spec2 = pl.BlockSpec((2 * rows, RW_WIDTH), lambda i: (i, 0))
    wr, mb, uloc, oloc, kvloc, gcol = pl.pallas_call(
        functools.partial(_rw_intra_kernel, n_sub=n_sub),
        grid=(t // rows,),
        in_specs=[spec] * 8,
        out_specs=[spec2, spec2, spec, spec, spec, spec],
        out_shape=[jax.ShapeDtypeStruct((2 * t, RW_WIDTH), BF16)] * 2 + [jax.ShapeDtypeStruct((t, RW_WIDTH), F32)] * 4,
        compiler_params=_cparams(1),
        name="rw_intra",
    )(*ops)
    n_chunk = seq // RW_CHUNK
    b3 = lambda a: a.reshape(n_batch, -1, RW_WIDTH)
    cspec = pl.BlockSpec((n_batch, RW_CHUNK, RW_WIDTH), lambda c: (0, c, 0))
    cspec2 = pl.BlockSpec((n_batch, 2 * RW_CHUNK, RW_WIDTH), lambda c: (0, c, 0))
    y = pl.pallas_call(
        functools.partial(_rw_seq_kernel, n_batch=n_batch),
        grid=(n_chunk,),
        in_specs=[cspec2, cspec2, cspec, cspec, cspec, cspec],
        out_specs=cspec,
        out_shape=jax.ShapeDtypeStruct((n_batch, seq, RW_WIDTH), F32),
        scratch_shapes=[pltpu.VMEM((n_batch, RW_HEAD, RW_WIDTH), F32)],
        compiler_params=_cparams(1),
        name="rw_seq",
    )(b3(wr), b3(mb), b3(uloc), b3(oloc), b3(kvloc), b3(gcol))
    return y.reshape(t, RW_WIDTH)


def _merge_kernel(x_ref, ys5_ref, us5_ref, yb_ref, yrw_ref, bonus_ref, g_ref, gl_ref, d_ref, wglu_ref, lng_ref, lnb_ref,
                  e64_ref, wb_ref, wout_ref, o_ref):
    ya = ys5_ref[...] + d_ref[...] * us5_ref[...]
    ya = 0.5 * ya * (1.0 + jnp.tanh(math.sqrt(2.0 / math.pi) * (ya + 0.044715 * (ya * ya * ya))))
    ya = ya * _sigmoid(_mm(ya, wglu_ref[...]))
    y = yrw_ref[...]
    mean = _mm_sel_right(y, e64_ref[...], GROUP_SUM_TERMS) * (1.0 / RW_HEAD)
    yc = y - mean
    var = _mm_sel_right(yc * yc, e64_ref[...], GROUP_SUM_TERMS) * (1.0 / RW_HEAD)
    yc = yc * lax.rsqrt(var + RW_LN_EPS) * lng_ref[...] + lnb_ref[...]
    yc = (yc + bonus_ref[...]) * g_ref[...]
    merged = jnp.zeros((x_ref.shape[0], D_MODEL), F32)
    for n, br in enumerate((ya, yb_ref[...], yc)):
        gate = _sigmoid(gl_ref[:, n * D_MODEL:(n + 1) * D_MODEL])
        merged = merged + gate * _mm(br, wb_ref[n])
    o_ref[...] = x_ref[...] + _mm(merged, wout_ref[...])


def _merge(x2, ys5, us5, yb, yrw, bonus, g, gl, s5_d, w_glu, ln_g, ln_b, w_branch, w_out, tm):
    t = x2.shape[0]
    consts = [s5_d[None, :], w_glu.astype(BF16), ln_g[None, :], ln_b[None, :], _group_ones(RW_WIDTH, RW_HEAD),
              w_branch.astype(BF16), w_out.astype(BF16)]
    row = lambda width: pl.BlockSpec((tm, width), lambda i: (i, 0))
    widths = [D_MODEL, S5_WIDTH, S5_WIDTH, DA_WIDTH, RW_WIDTH, RW_WIDTH, RW_WIDTH, N_BRANCH * D_MODEL]
    return pl.pallas_call(
        _merge_kernel,
        grid=(t // tm,),
        in_specs=[row(wd) for wd in widths] + [_const_spec(c.shape) for c in consts],
        out_specs=row(D_MODEL),
        out_shape=jax.ShapeDtypeStruct((t, D_MODEL), F32),
        compiler_params=_cparams(1),
        name="merge",
    )(x2, ys5, us5, yb, yrw, bonus, g, gl, *consts)


def _ffn_kernel(x_ref, g_ref, win_ref, wout_ref, o_ref):
    x = x_ref[...]
    h = (x * lax.rsqrt(jnp.mean(x * x, axis=-1, keepdims=True) + NORM_EPS) * g_ref[...]).astype(BF16)
    gate = jnp.dot(h, win_ref[:, :D_FF], preferred_element_type=F32)
    up = jnp.dot(h, win_ref[:, D_FF:], preferred_element_type=F32)
    act = gate * _sigmoid(gate) * up
    o_ref[...] = x + _mm(act, wout_ref[...])


def _ffn(x2, g, w_in, w_out, tm):
    t = x2.shape[0]
    consts = [g[None, :], w_in.astype(BF16), w_out.astype(BF16)]
    row = pl.BlockSpec((tm, D_MODEL), lambda i: (i, 0))
    return pl.pallas_call(
        _ffn_kernel,
        grid=(t // tm,),
        in_specs=[row] + [_const_spec(c.shape) for c in consts],
        out_specs=row,
        out_shape=jax.ShapeDtypeStruct((t, D_MODEL), F32),
        compiler_params=_cparams(1),
        name="ffn",
    )(x2, *consts)


def _layer(x2, i, n_batch, seq, p, v_first, tm):
    us5, q, k, v, zrw, gl = _in_proj(x2, p['norm1_g'], p['w_in'], p['da_q_gain'], p['da_k_gain'], tm)
    n_double = max(1, (seq // S5_CHUNK - 1).bit_length())
    tables = _s5_tables(p['s5_lambda_re'], p['s5_lambda_im'], p['s5_log_dt'], p['s5_b_re'], p['s5_b_im'],
                        p['s5_c_re'], p['s5_c_im'], n_double)
    ys5 = _s5_scan(us5, tables, n_batch, seq)
    lambda_init = 0.8 - 0.6 * math.exp(-0.3 * i)
    yb = _diff_attn(q, k, v, p['da_lambda'], p['da_subln_g'], lambda_init, n_batch, seq)
    rw = {name[3:]: val for name, val in p.items() if name.startswith('rw_')}
    outs = _rw_prep(zrw, rw, v_first, seq, min(RW_PREP_TILE, seq))
    if v_first is None:
        v_first = outs[10]
    yrw = _rw_scan(outs[:8], n_batch, seq)
    x2 = _merge(x2, ys5, us5, yb, yrw, outs[8], outs[9], gl, p['s5_d'], p['s5_w_glu'], p['rw_ln_g'], p['rw_ln_b'],
                p['w_branch'], p['w_out'], tm)
    x2 = _ffn(x2, p['norm2_g'], p['w_ffn_in'], p['w_ffn_out'], tm)
    return x2, v_first


def kernel(x, norm1_g, w_in, s5_lambda_re, s5_lambda_im, s5_log_dt, s5_b_re, s5_b_im, s5_c_re, s5_c_im, s5_d, s5_w_glu, da_q_gain, da_k_gain, da_lambda, da_subln_g, rw_mu, rw_w0, rw_w2, rw_a0, rw_a2, rw_g2, rw_k_k, rw_k_a, rw_r_k, rw_ln_g, rw_ln_b, rw_v0, rw_v1, rw_v2, w_branch, w_out, norm2_g, w_ffn_in, w_ffn_out):
    n_batch, seq, _ = x.shape
    per_layer = dict(norm1_g=norm1_g, w_in=w_in, s5_lambda_re=s5_lambda_re, s5_lambda_im=s5_lambda_im,
                     s5_log_dt=s5_log_dt, s5_b_re=s5_b_re, s5_b_im=s5_b_im, s5_c_re=s5_c_re, s5_c_im=s5_c_im,
                     s5_d=s5_d, s5_w_glu=s5_w_glu, da_q_gain=da_q_gain, da_k_gain=da_k_gain, da_lambda=da_lambda,
                     da_subln_g=da_subln_g, rw_mu=rw_mu, rw_w0=rw_w0, rw_w2=rw_w2, rw_a0=rw_a0, rw_a2=rw_a2,
                     rw_g2=rw_g2, rw_k_k=rw_k_k, rw_k_a=rw_k_a, rw_r_k=rw_r_k, rw_ln_g=rw_ln_g, rw_ln_b=rw_ln_b,
                     w_branch=w_branch, w_out=w_out, norm2_g=norm2_g, w_ffn_in=w_ffn_in, w_ffn_out=w_ffn_out)
    tm = min(TOKEN_TILE, seq)
    x2 = x.reshape(n_batch * seq, D_MODEL)
    v_first = None
    for i in range(w_in.shape[0]):
        p = {name: val[i] for name, val in per_layer.items()}
        if i > 0:
            p.update(rw_v0=rw_v0[i - 1], rw_v1=rw_v1[i - 1], rw_v2=rw_v2[i - 1])
        x2, v_first = _layer(x2, i, n_batch, seq, p, v_first, tm)
    return x2.reshape(x.shape)
```
